```python
import math
import jax, jax.numpy as jnp
from jax import lax
import numpy as np

D_MODEL = 1024
BATCH = 2
SEQ = 8192
DEPTH = 4

N_MEM = 256
EPS = 1e-6
CONV_W = 512
CONV_K = 31
POOL_W = 512
POOL_GROUPS = 4
POOL_GW = POOL_W // POOL_GROUPS
POOL_WINDOWS = (2, 4, 8, 16)
MLSTM_W = 1024
MLSTM_HEADS = 4
MLSTM_DH = MLSTM_W // MLSTM_HEADS
MLSTM_CONV_K = 4
MLSTM_CHUNK = 64
N_BRANCH = 3
IN_COLS = 2 * CONV_W + POOL_W + 4 * MLSTM_W + 2 * MLSTM_HEADS + N_BRANCH * D_MODEL
XA_HEADS = 4
XA_DH = D_MODEL // XA_HEADS
D_FF = 2816
N_EXPERTS = 8
TOP_K = 2
D_FF_EXPERT = 3584
MOE_BLOCK = 128
N_DENSE = (DEPTH + 1) // 2
N_MOE = DEPTH // 2

kernel_name = "hybrid_conv_pool_mlstm_xattn_moe"


def rms_norm(x, g):
    x32 = x.astype(jnp.float32)
    y = x32 * lax.rsqrt(jnp.mean(x32 * x32, axis=-1, keepdims=True) + EPS)
    return (y * g.astype(jnp.float32)).astype(x.dtype)


def layer_norm(x, g, b):
    x32 = x.astype(jnp.float32)
    mu = jnp.mean(x32, axis=-1, keepdims=True)
    var = jnp.mean(jnp.square(x32 - mu), axis=-1, keepdims=True)
    y = (x32 - mu) * lax.rsqrt(var + EPS)
    return (y * g.astype(jnp.float32) + b.astype(jnp.float32)).astype(x.dtype)


def causal_depthwise_conv(u, w, b):
    K, C = w.shape
    y = lax.conv_general_dilated(
        u, w[:, None, :], window_strides=(1,), padding=[(K - 1, 0)],
        dimension_numbers=("NWC", "WIO", "NWC"), feature_group_count=C)
    return y + b


def causal_mean_minus_self(u, window):
    S = u.shape[1]
    cs = jnp.cumsum(u, axis=1)
    prev = jnp.pad(cs, ((0, 0), (window, 0), (0, 0)))[:, :S]
    cnt = jnp.minimum(jnp.arange(1, S + 1), window).astype(jnp.float32)
    return (cs - prev) / cnt[None, :, None] - u


def pool_mixer(u, pool_w, pool_scale):
    Bsz, S, _ = u.shape
    u32 = u.astype(jnp.float32).reshape(Bsz, S, POOL_GROUPS, POOL_GW)
    pooled = jnp.stack([causal_mean_minus_self(u32[:, :, g], POOL_WINDOWS[g])
                        for g in range(POOL_GROUPS)], axis=2)
    y = jnp.einsum("bsgc,gcd->bsgd", pooled, pool_w.astype(jnp.float32))
    y = y.reshape(Bsz, S, POOL_W) * pool_scale.astype(jnp.float32)
    return y.astype(u.dtype)


def mlstm_chunkwise(q, k, v, ig, lf):
    Bsz, H, S, dh = q.shape
    L = MLSTM_CHUNK
    nc = S // L

    def chunks(a):
        return jnp.moveaxis(a.reshape(Bsz, H, nc, L, *a.shape[3:]), 2, 0)

    causal = jnp.tril(jnp.ones((L, L), dtype=bool))

    def step(carry, xs):
        C, n, m = carry
        q_, k_, v_, i_, f_ = xs
        b = jnp.cumsum(f_, axis=-1)
        logD = b[..., :, None] - b[..., None, :] + i_[..., None, :]
        logD = jnp.where(causal, logD, -jnp.inf)
        inter = b + m[..., None]
        mt = jnp.maximum(inter, jnp.max(logD, axis=-1))
        Dm = jnp.exp(logD - mt[..., None])
        a_inter = jnp.exp(inter - mt)
        s = jnp.einsum("bhtd,bhsd->bhts", q_, k_) * Dm
        num = (jnp.einsum("bhts,bhse->bhte", s, v_)
               + a_inter[..., None] * jnp.einsum("bhed,bhtd->bhte", C, q_))
        den = jnp.sum(s, axis=-1) + a_inter * jnp.einsum("bhd,bhtd->bht", n, q_)
        h = num / jnp.maximum(jnp.abs(den), jnp.exp(-mt))[..., None]
        bL = b[..., -1]
        w_log = bL[..., None] - b + i_
        m_new = jnp.maximum(bL + m, jnp.max(w_log, axis=-1))
        a_c = jnp.exp(bL + m - m_new)
        w = jnp.exp(w_log - m_new[..., None])
        C_new = a_c[..., None, None] * C + jnp.einsum("bhs,bhse,bhsd->bhed", w, v_, k_)
        n_new = a_c[..., None] * n + jnp.einsum("bhs,bhsd->bhd", w, k_)
        return (C_new, n_new, m_new), h

    init = (jnp.zeros((Bsz, H, dh, dh), jnp.float32),
            jnp.zeros((Bsz, H, dh), jnp.float32),
            jnp.zeros((Bsz, H), jnp.float32))
    _, hs = lax.scan(step, init, (chunks(q), chunks(k), chunks(v), chunks(ig), chunks(lf)))
    return jnp.moveaxis(hs, 0, 2).reshape(Bsz, H, S, dh)


def hybrid_mixer(xn, w_in, b_in, conv_dw_w, conv_dw_b, conv_ln_g, conv_ln_b, w_conv_out,
                 pool_w, pool_scale, w_pool_out, mlstm_conv_w, mlstm_conv_b,
                 mlstm_norm_g, w_mlstm_out, w_out):
    Bsz, S, D = xn.shape
    dt = xn.dtype
    proj = xn @ w_in + b_in
    sizes = [CONV_W, CONV_W, POOL_W, MLSTM_W, MLSTM_W, MLSTM_W, MLSTM_W,
             MLSTM_HEADS, MLSTM_HEADS, N_BRANCH * D_MODEL]
    offs, acc = [], 0
    for sz in sizes[:-1]:
        acc += sz
        offs.append(acc)
    c_a, c_b, p_u, m_q, m_k, m_v, m_z, m_i, m_f, g_pre = jnp.split(proj, offs, axis=-1)

    u = c_a * jax.nn.sigmoid(c_b)
    u = causal_depthwise_conv(u, conv_dw_w, conv_dw_b)
    u = jax.nn.silu(layer_norm(u, conv_ln_g, conv_ln_b))
    y_a = u @ w_conv_out

    y_b = pool_mixer(p_u, pool_w, pool_scale) @ w_pool_out

    qk = jax.nn.silu(causal_depthwise_conv(jnp.concatenate([m_q, m_k], axis=-1),
                                           mlstm_conv_w, mlstm_conv_b))
    q, k = jnp.split(qk, 2, axis=-1)

    def heads(t):
        return t.reshape(Bsz, S, MLSTM_HEADS, MLSTM_DH).transpose(0, 2, 1, 3).astype(jnp.float32)

    ig = m_i.astype(jnp.float32).transpose(0, 2, 1)
    lf = jax.nn.log_sigmoid(m_f.astype(jnp.float32)).transpose(0, 2, 1)
    h = mlstm_chunkwise(heads(q), heads(k) * (MLSTM_DH ** -0.5), heads(v=m_v) if False else heads(m_v), ig, lf)
    h = h.transpose(0, 2, 1, 3)
    h = h * lax.rsqrt(jnp.mean(h * h, axis=-1, keepdims=True) + EPS)
    h = h * mlstm_norm_g.astype(jnp.float32).reshape(MLSTM_HEADS, MLSTM_DH)
    h = h.reshape(Bsz, S, MLSTM_W) * jax.nn.sigmoid(m_z.astype(jnp.float32))
    y_c = h.astype(dt) @ w_mlstm_out

    g = jax.nn.sigmoid(g_pre.astype(jnp.float32)).astype(dt).reshape(Bsz, S, N_BRANCH, D_MODEL)
    y = g[:, :, 0] * y_a + g[:, :, 1] * y_b + g[:, :, 2] * y_c
    return y @ w_out


def memory_cross_attention(hn, mem_n, wq, wk, wv, wo):
    Bsz, S, _ = hn.shape
    M = mem_n.shape[1]
    q = (hn @ wq).reshape(Bsz, S, XA_HEADS, XA_DH)
    k = (mem_n @ wk).reshape(Bsz, M, XA_HEADS, XA_DH)
    v = (mem_n @ wv).reshape(Bsz, M, XA_HEADS, XA_DH)
    s = jnp.einsum("bshd,bmhd->bhsm", q, k).astype(jnp.float32) * (XA_DH ** -0.5)
    p = jax.nn.softmax(s, axis=-1).astype(v.dtype)
    o = jnp.einsum("bhsm,bmhd->bshd", p, v).reshape(Bsz, S, XA_HEADS * XA_DH)
    return o @ wo


def swiglu(xn, w1, w3, w2):
    return (jax.nn.silu(xn @ w1) * (xn @ w3)) @ w2


def moe_swiglu(xn, router_w, w1, w3, w2):
    Bsz, S, D = xn.shape
    N = Bsz * S
    xt = xn.reshape(N, D)
    logits = (xt @ router_w).astype(jnp.float32)
    top_v, top_e = lax.top_k(logits, TOP_K)
    gate = jax.nn.softmax(top_v, axis=-1)
    A = N * TOP_K
    e_flat = top_e.reshape(A)
    tok_flat = jnp.repeat(jnp.arange(N, dtype=jnp.int32), TOP_K)
    w_flat = gate.reshape(A)
    order = jnp.argsort(e_flat)
    e_s, tok_s, w_s = e_flat[order], tok_flat[order], w_flat[order]
    counts = jnp.bincount(e_flat, length=N_EXPERTS)
    padded = (counts + MOE_BLOCK - 1) // MOE_BLOCK * MOE_BLOCK
    off = jnp.cumsum(counts) - counts
    ends = jnp.cumsum(padded)
    poff = ends - padded
    dest = poff[e_s] + (jnp.arange(A, dtype=jnp.int32) - off[e_s])
    n_blocks = -(-A // MOE_BLOCK) + N_EXPERTS
    P = n_blocks * MOE_BLOCK
    slot_tok = jnp.zeros((P,), jnp.int32).at[dest].set(tok_s)
    slot_w = jnp.zeros((P,), jnp.float32).at[dest].set(w_s)
    block_e = jnp.minimum(
        jnp.searchsorted(ends, jnp.arange(n_blocks) * MOE_BLOCK, side="right"), N_EXPERTS - 1)
    xb = xt[slot_tok].reshape(n_blocks, MOE_BLOCK, D)

    def expert_block(args):
        xblk, e = args
        return (jax.nn.silu(xblk @ w1[e]) * (xblk @ w3[e])) @ w2[e]

    yb = lax.map(expert_block, (xb, block_e)).reshape(P, D)
    y = jax.ops.segment_sum(yb * slot_w[:, None], slot_tok, num_segments=N)
    return y.reshape(Bsz, S, D).astype(xn.dtype)


def setup_inputs(seed: int = 0) -> dict:
    key = jax.random.key(seed)
    ks = iter(jax.random.split(key, 48))
    f32 = jnp.float32

    def nrm(shape, scale):
        return jax.random.normal(next(ks), shape, f32) * scale

    def gain(shape):
        return 1.0 + 0.05 * jax.random.normal(next(ks), shape, f32)

    D = D_MODEL
    f_off = 2 * CONV_W + POOL_W + 4 * MLSTM_W + MLSTM_HEADS
    b_in = nrm((DEPTH, IN_COLS), 0.02)
    b_in = b_in.at[:, f_off:f_off + MLSTM_HEADS].add(jnp.linspace(3.0, 6.0, MLSTM_HEADS))
    return {
        "x": nrm((BATCH, SEQ, D), 1.0),
        "mem": nrm((BATCH, N_MEM, D), 1.0),
        "mem_norm_g": gain((D,)),
        "norm_mix_g": gain((DEPTH, D)),
        "w_in": nrm((DEPTH, D, IN_COLS), D ** -0.5),
        "b_in": b_in,
        "conv_dw_w": nrm((DEPTH, CONV_K, CONV_W), CONV_K ** -0.5),
        "conv_dw_b": nrm((DEPTH, CONV_W), 0.02),
        "conv_ln_g": gain((DEPTH, CONV_W)),
        "conv_ln_b": nrm((DEPTH, CONV_W), 0.02),
        "w_conv_out": nrm((DEPTH, CONV_W, D), CONV_W ** -0.5),
        "pool_w": nrm((DEPTH, POOL_GROUPS, POOL_GW, POOL_GW), POOL_GW ** -0.5),
        "pool_scale": 1.0 + 0.1 * nrm((DEPTH, POOL_W), 1.0),
        "w_pool_out": nrm((DEPTH, POOL_W, D), POOL_W ** -0.5),
        "mlstm_conv_w": nrm((DEPTH, MLSTM_CONV_K, 2 * MLSTM_W), MLSTM_CONV_K ** -0.5),
        "mlstm_conv_b": nrm((DEPTH, 2 * MLSTM_W), 0.02),
        "mlstm_norm_g": gain((DEPTH, MLSTM_W)),
        "w_mlstm_out": nrm((DEPTH, MLSTM_W, D), MLSTM_W ** -0.5),
        "w_out": nrm((DEPTH, D, D), D ** -0.5),
        "norm_xattn_g": gain((DEPTH, D)),
        "xa_wq": nrm((DEPTH, D, XA_HEADS * XA_DH), D ** -0.5),
        "xa_wk": nrm((DEPTH, D, XA_HEADS * XA_DH), D ** -0.5),
        "xa_wv": nrm((DEPTH, D, XA_HEADS * XA_DH), D ** -0.5),
        "xa_wo": nrm((DEPTH, XA_HEADS * XA_DH, D), (XA_HEADS * XA_DH) ** -0.5),
        "norm_ffn_g": gain((DEPTH, D)),
        "ffn_w1": nrm((N_DENSE, D, D_FF), D ** -0.5),
        "ffn_w3": nrm((N_DENSE, D, D_FF), D ** -0.5),
        "ffn_w2": nrm((N_DENSE, D_FF, D), D_FF ** -0.5),
        "router_w": nrm((N_MOE, D, N_EXPERTS), D ** -0.5),
        "moe_w1": nrm((N_MOE, N_EXPERTS, D, D_FF_EXPERT), D ** -0.5),
        "moe_w3": nrm((N_MOE, N_EXPERTS, D, D_FF_EXPERT), D ** -0.5),
        "moe_w2": nrm((N_MOE, N_EXPERTS, D_FF_EXPERT, D), D_FF_EXPERT ** -0.5),
        "final_norm_g": gain((D,)),
    }


def reference(x, mem, mem_norm_g, norm_mix_g, w_in, b_in, conv_dw_w, conv_dw_b,
              conv_ln_g, conv_ln_b, w_conv_out, pool_w, pool_scale, w_pool_out,
              mlstm_conv_w, mlstm_conv_b, mlstm_norm_g, w_mlstm_out, w_out,
              norm_xattn_g, xa_wq, xa_wk, xa_wv, xa_wo, norm_ffn_g,
              ffn_w1, ffn_w3, ffn_w2, router_w, moe_w1, moe_w3, moe_w2, final_norm_g):
    mem_n = rms_norm(mem, mem_norm_g)
    h = x
    for l in range(DEPTH):
        h = h + hybrid_mixer(rms_norm(h, norm_mix_g[l]), w_in[l], b_in[l],
                             conv_dw_w[l], conv_dw_b[l], conv_ln_g[l], conv_ln_b[l],
                             w_conv_out[l], pool_w[l], pool_scale[l], w_pool_out[l],
                             mlstm_conv_w[l], mlstm_conv_b[l], mlstm_norm_g[l],
                             w_mlstm_out[l], w_out[l])
        h = h + memory_cross_attention(rms_norm(h, norm_xattn_g[l]), mem_n,
                                       xa_wq[l], xa_wk[l], xa_wv[l], xa_wo[l])
        hn = rms_norm(h, norm_ffn_g[l])
        j = l // 2
        if l % 2 == 0:
            h = h + swiglu(hn, ffn_w1[j], ffn_w3[j], ffn_w2[j])
        else:
            h = h + moe_swiglu(hn, router_w[j], moe_w1[j], moe_w3[j], moe_w2[j])
    return rms_norm(h, final_norm_g)
```

```python
import functools

import jax
import jax.numpy as jnp
from jax import lax
from jax.experimental import pallas as pl
from jax.experimental.pallas import tpu as pltpu

F32 = jnp.float32
BF16 = jnp.bfloat16
EPS = 1e-6
NEG = -1e30

D_MODEL = 1024
CONV_W = 512
CONV_K = 31
POOL_W = 512
POOL_GW = 128
POOL_WINDOWS = (2, 4, 8, 16)
MLSTM_W = 1024
HEADS = 4
DH = 256
MLSTM_CONV_K = 4
XA_HEADS = 4
XA_DH = 256
N_EXPERTS = 8
TOP_K = 2

LANES = 128
SUBLANES = 8
VMEM_LIMIT = 56 * 1024 * 1024

ROW_TILE = 512
CONV_HALO = 32
POOL_HALO = 16
QK_HALO = 8
VPU_ROWS = 64
MLSTM_CHUNK = 256
MOE_BLOCK = 512
FF_CHUNK_DENSE = 1408
FF_CHUNK_MOE = 896
ROUTE_TILE = 512
COPY_ROWS = 512

_C_A = 0
_C_B = _C_A + CONV_W
_P_U = _C_B + CONV_W
_M_QK = _P_U + POOL_W
_M_V = _M_QK + 2 * MLSTM_W
_M_Z = _M_V + MLSTM_W
_M_IF = _M_Z + MLSTM_W
_G_PRE = _M_IF + 2 * HEADS


def _params(n_axes=1):
    return pltpu.CompilerParams(dimension_semantics=("arbitrary",) * n_axes, vmem_limit_bytes=VMEM_LIMIT)


def _resident(block_shape, index_map):
    return pl.BlockSpec(block_shape, index_map, pipeline_mode=pl.Buffered(1))


def _rms(x, g):
    return x * lax.rsqrt(jnp.mean(x * x, axis=-1, keepdims=True) + EPS) * g


def _sigmoid(x):
    return 1.0 / (1.0 + jnp.exp(-x))


def _dot(a, b):
    return jnp.dot(a, b, preferred_element_type=F32)


def _inproj_body(h_ref, g_ref, w_ref, b_ref, wif_ref, bif_ref,
                 u_ref, pu_ref, qk_ref, v_ref, z_ref, if_ref):
    xn = _rms(h_ref[...], g_ref[...]).astype(BF16)

    def proj(lo, hi):
        return _dot(xn, w_ref[:, lo:hi]) + b_ref[:, lo:hi]

    u_ref[...] = proj(_C_A, _C_B) * _sigmoid(proj(_C_B, _P_U))
    pu_ref[...] = proj(_P_U, _M_QK)
    qk_ref[...] = proj(_M_QK, _M_V)
    v_ref[...] = proj(_M_V, _M_Z).astype(BF16)
    z_ref[...] = proj(_M_Z, _M_IF)
    if_ref[...] = _dot(xn, wif_ref[...]) + bif_ref[...]


def _inproj(h, gmix, w_main, b_main, w_if, b_if, l):
    n = h.shape[0]
    t = ROW_TILE
    lsel = lambda i: (l, 0, 0)
    row = lambda i: (i, 0)
    return pl.pallas_call(
        _inproj_body,
        grid=(n // t,),
        in_specs=[
            pl.BlockSpec((t, D_MODEL), row),
            _resident((None, 1, D_MODEL), lsel),
            _resident((None, D_MODEL, _M_IF), lsel),
            _resident((None, 1, _M_IF), lsel),
            _resident((None, D_MODEL, LANES), lsel),
            _resident((None, 1, LANES), lsel),
        ],
        out_specs=[
            pl.BlockSpec((t, CONV_W), row),
            pl.BlockSpec((t, POOL_W), row),
            pl.BlockSpec((t, 2 * MLSTM_W), row),
            pl.BlockSpec((t, MLSTM_W), row),
            pl.BlockSpec((t, MLSTM_W), row),
            pl.BlockSpec((t, LANES), row),
        ],
        out_shape=[
            jax.ShapeDtypeStruct((n, CONV_W), F32),
            jax.ShapeDtypeStruct((n, POOL_W), F32),
            jax.ShapeDtypeStruct((n, 2 * MLSTM_W), F32),
            jax.ShapeDtypeStruct((n, MLSTM_W), BF16),
            jax.ShapeDtypeStruct((n, MLSTM_W), F32),
            jax.ShapeDtypeStruct((n, LANES), F32),
        ],
        compiler_params=_params(1),
        name="inproj",
    )(h, gmix, w_main, b_main, w_if, b_if)


def _branches_body(h_ref, g_ref, u_ref, uprev_ref, pu_ref, puprev_ref,
                   cw_ref, cb_ref, lng_ref, lnb_ref, wco_ref,
                   pw_ref, ps_ref, wpo_ref, wg_ref, bg_ref,
                   yab_ref, extu, extp, convs, pools):
    t = u_ref.shape[0]
    i = pl.program_id(1)
    has_prev = i > 0

    extu[0:CONV_HALO, :] = jnp.where(has_prev, uprev_ref[...], 0.0)
    extu[CONV_HALO:CONV_HALO + t, :] = u_ref[...]
    base = CONV_HALO - (CONV_K - 1)
    for r0 in range(0, t, VPU_ROWS):
        for c0 in range(0, CONV_W, LANES):
            acc = jnp.broadcast_to(cb_ref[:, c0:c0 + LANES], (VPU_ROWS, LANES))
            for j in range(CONV_K):
                acc = acc + cw_ref[j:j + 1, c0:c0 + LANES] * extu[r0 + base + j:r0 + base + j + VPU_ROWS, c0:c0 + LANES]
            convs[r0:r0 + VPU_ROWS, c0:c0 + LANES] = acc
    cv = convs[...]
    mu = jnp.mean(cv, axis=-1, keepdims=True)
    var = jnp.mean(jnp.square(cv - mu), axis=-1, keepdims=True)
    a = (cv - mu) * lax.rsqrt(var + EPS) * lng_ref[...] + lnb_ref[...]
    a = a * _sigmoid(a)
    ya = _dot(a.astype(BF16), wco_ref[...])

    extp[0:POOL_HALO, :] = jnp.where(has_prev, puprev_ref[...], 0.0)
    extp[POOL_HALO:POOL_HALO + t, :] = pu_ref[...]
    for g, window in enumerate(POOL_WINDOWS):
        c0 = g * POOL_GW
        for r0 in range(0, t, VPU_ROWS):
            own = extp[POOL_HALO + r0:POOL_HALO + r0 + VPU_ROWS, c0:c0 + POOL_GW]
            acc = own
            for k in range(1, window):
                acc = acc + extp[POOL_HALO + r0 - k:POOL_HALO + r0 - k + VPU_ROWS, c0:c0 + POOL_GW]
            pos = i * t + r0 + lax.broadcasted_iota(jnp.int32, (VPU_ROWS, POOL_GW), 0)
            cnt = jnp.minimum(pos + 1, window).astype(F32)
            pools[r0:r0 + VPU_ROWS, c0:c0 + POOL_GW] = acc / cnt - own
    yp = jnp.concatenate(
        [_dot(pools[:, g * POOL_GW:(g + 1) * POOL_GW].astype(BF16), pw_ref[g]) for g in range(len(POOL_WINDOWS))],
        axis=1) * ps_ref[...]
    yb = _dot(yp.astype(BF16), wpo_ref[...])

    xn = _rms(h_ref[...], g_ref[...]).astype(BF16)
    gates = _sigmoid(_dot(xn, wg_ref[...]) + bg_ref[...])
    yab_ref[...] = gates[:, :D_MODEL] * ya + gates[:, D_MODEL:] * yb


def _branches(h, gmix, u, pu, conv_w, conv_b, ln_g, ln_b, w_conv_out, pool_w, pool_scale, w_pool_out,
              w_gate, b_gate, l):
    bsz, s, _ = h.shape
    t = ROW_TILE
    lsel3 = lambda b, i: (l, 0, 0)
    lsel4 = lambda b, i: (l, 0, 0, 0)
    cur = lambda b, i: (b, i, 0)
    prev_u = lambda b, i: (b, jnp.maximum(i * (t // CONV_HALO) - 1, 0), 0)
    prev_p = lambda b, i: (b, jnp.maximum(i * (t // POOL_HALO) - 1, 0), 0)
    return pl.pallas_call(
        _branches_body,
        grid=(bsz, s // t),
        in_specs=[
            pl.BlockSpec((None, t, D_MODEL), cur),
            _resident((None, 1, D_MODEL), lsel3),
            pl.BlockSpec((None, t, CONV_W), cur),
            pl.BlockSpec((None, CONV_HALO, CONV_W), prev_u),
            pl.BlockSpec((None, t, POOL_W), cur),
            pl.BlockSpec((None, POOL_HALO, POOL_W), prev_p),
            _resident((None, CONV_K, CONV_W), lsel3),
            _resident((None, 1, CONV_W), lsel3),
            _resident((None, 1, CONV_W), lsel3),
            _resident((None, 1, CONV_W), lsel3),
            _resident((None, CONV_W, D_MODEL), lsel3),
            _resident((None, len(POOL_WINDOWS), POOL_GW, POOL_GW), lsel4),
            _resident((None, 1, POOL_W), lsel3),
            _resident((None, POOL_W, D_MODEL), lsel3),
            _resident((None, D_MODEL, 2 * D_MODEL), lsel3),
            _resident((None, 1, 2 * D_MODEL), lsel3),
        ],
        out_specs=pl.BlockSpec((None, t, D_MODEL), cur),
        out_shape=jax.ShapeDtypeStruct((bsz, s, D_MODEL), F32),
        scratch_shapes=[
            pltpu.VMEM((CONV_HALO + t, CONV_W), F32),
            pltpu.VMEM((POOL_HALO + t, POOL_W), F32),
            pltpu.VMEM((t, CONV_W), F32),
            pltpu.VMEM((t, POOL_W), F32),
        ],
        compiler_params=_params(2),
        name="branches",
    )(h, gmix, u, u, pu, pu, conv_w, conv_b, ln_g, ln_b, w_conv_out, pool_w, pool_scale, w_pool_out,
      w_gate, b_gate)


def _cumsum_rows(x):
    n = x.shape[0]
    row = lax.broadcasted_iota(jnp.int32, x.shape, 0)
    shift = 1
    while shift < n:
        x = x + jnp.where(row >= shift, pltpu.roll(x, shift, axis=0), 0.0)
        shift *= 2
    return x


def _mlstm_body(qk_ref, qkprev_ref, v_ref, z_ref, if_ref, cw_ref, cb_ref, ng_ref,
                hz_ref, ext, ct_scr, n_scr, m_scr):
    ln = qk_ref.shape[0]
    c = pl.program_id(1)

    @pl.when(c == 0)
    def _():
        ct_scr[...] = jnp.zeros_like(ct_scr)
        n_scr[...] = jnp.zeros_like(n_scr)
        m_scr[...] = jnp.zeros_like(m_scr)

    ext[0:QK_HALO, :] = jnp.where(c > 0, qkprev_ref[...], 0.0)
    ext[QK_HALO:QK_HALO + ln, :] = qk_ref[...]
    base = QK_HALO - (MLSTM_CONV_K - 1)
    conv = jnp.broadcast_to(cb_ref[...], (ln, 2 * MLSTM_W))
    for j in range(MLSTM_CONV_K):
        conv = conv + cw_ref[j:j + 1, :] * ext[base + j:base + j + ln, :]
    qk = conv * _sigmoid(conv)

    gates = if_ref[...]
    logf = jnp.minimum(gates, 0.0) - jnp.log(1.0 + jnp.exp(-jnp.abs(gates)))
    bcum = _cumsum_rows(logf)
    lane = lax.broadcasted_iota(jnp.int32, gates.shape, 1)
    rows_t = jnp.where(lane < HEADS, gates, bcum).T

    tt = lax.broadcasted_iota(jnp.int32, (ln, ln), 0)
    ss = lax.broadcasted_iota(jnp.int32, (ln, ln), 1)
    causal = ss <= tt

    for hd in range(HEADS):
        q = qk[:, hd * DH:(hd + 1) * DH]
        k = qk[:, MLSTM_W + hd * DH:MLSTM_W + (hd + 1) * DH] * (DH ** -0.5)
        v = v_ref[:, hd * DH:(hd + 1) * DH]
        i_col = gates[:, hd:hd + 1]
        b_col = bcum[:, HEADS + hd:HEADS + hd + 1]
        i_row = rows_t[hd:hd + 1, :]
        b_row = rows_t[HEADS + hd:HEADS + hd + 1, :]
        m_prev = m_scr[hd][:, 0:1]
        ct = ct_scr[hd]
        nvec = n_scr[hd]

        log_d = jnp.where(causal, b_col - b_row + i_row, NEG)
        inter = b_col + m_prev
        mt = jnp.maximum(inter, jnp.max(log_d, axis=-1, keepdims=True))
        dmat = jnp.exp(log_d - mt)
        a_inter = jnp.exp(inter - mt)
        qb = q.astype(BF16)
        kb = k.astype(BF16)
        sc = lax.dot_general(qb, kb, (((1,), (1,)), ((), ())), preferred_element_type=F32) * dmat
        num = _dot(sc.astype(BF16), v) + a_inter * _dot(qb, ct.astype(BF16))
        den = jnp.sum(sc, axis=-1, keepdims=True) + a_inter * jnp.sum(q * nvec, axis=-1, keepdims=True)
        hh = num / jnp.maximum(jnp.abs(den), jnp.exp(-mt))

        b_last = b_col[ln - 1:ln, :]
        w_log = b_last - b_col + i_col
        m_new = jnp.maximum(b_last + m_prev, jnp.max(w_log, axis=0, keepdims=True))
        a_c = jnp.exp(b_last + m_prev - m_new)
        kw = k * jnp.exp(w_log - m_new)
        ct_scr[hd] = a_c * ct + lax.dot_general(kw.astype(BF16), v, (((0,), (0,)), ((), ())),
                                                preferred_element_type=F32)
        n_scr[hd] = a_c * nvec + jnp.sum(kw, axis=0, keepdims=True)
        m_scr[hd] = jnp.broadcast_to(m_new, (1, LANES))

        hn = hh * lax.rsqrt(jnp.mean(hh * hh, axis=-1, keepdims=True) + EPS) * ng_ref[:, hd * DH:(hd + 1) * DH]
        hz_ref[:, hd * DH:(hd + 1) * DH] = (hn * _sigmoid(z_ref[:, hd * DH:(hd + 1) * DH])).astype(BF16)


def _mlstm(qk, v, z, gates_if, conv_w, conv_b, norm_g, l):
    bsz, s, _ = qk.shape
    ln = MLSTM_CHUNK
    lsel3 = lambda b, c: (l, 0, 0)
    cur = lambda b, c: (b, c, 0)
    prev = lambda b, c: (b, jnp.maximum(c * (ln // QK_HALO) - 1, 0), 0)
    return pl.pallas_call(
        _mlstm_body,
        grid=(bsz, s // ln),
        in_specs=[
            pl.BlockSpec((None, ln, 2 * MLSTM_W), cur),
            pl.BlockSpec((None, QK_HALO, 2 * MLSTM_W), prev),
            pl.BlockSpec((None, ln, MLSTM_W), cur),
            pl.BlockSpec((None, ln, MLSTM_W), cur),
            pl.BlockSpec((None, ln, LANES), cur),
            _resident((None, MLSTM_CONV_K, 2 * MLSTM_W), lsel3),
            _resident((None, 1, 2 * MLSTM_W), lsel3),
            _resident((None, 1, MLSTM_W), lsel3),
        ],
        out_specs=pl.BlockSpec((None, ln, MLSTM_W), cur),
        out_shape=jax.ShapeDtypeStruct((bsz, s, MLSTM_W), BF16),
        scratch_shapes=[
            pltpu.VMEM((QK_HALO + ln, 2 * MLSTM_W), F32),
            pltpu.VMEM((HEADS, DH, DH), F32),
            pltpu.VMEM((HEADS, 1, DH), F32),
            pltpu.VMEM((HEADS, 1, LANES), F32),
        ],
        compiler_params=_params(2),
        name="mlstm",
    )(qk, qk, v, z, gates_if, conv_w, conv_b, norm_g)


def _merge_xattn_body(with_router, h_ref, gmix_ref, yab_ref, hz_ref, wg_ref, bg_ref, wmo_ref, wout_ref,
                      gxa_ref, wq_ref, k_ref, v_ref, wo_ref, *rest):
    if with_router:
        gffn_ref, wr_ref, h2_ref, xf_ref, lg_ref = rest
    else:
        (h2_ref,) = rest
    h = h_ref[...]
    xn = _rms(h, gmix_ref[...]).astype(BF16)
    g2 = _sigmoid(_dot(xn, wg_ref[...]) + bg_ref[...])
    y = yab_ref[...] + g2 * _dot(hz_ref[...], wmo_ref[...])
    h1 = h + _dot(y.astype(BF16), wout_ref[...])

    hn = _rms(h1, gxa_ref[...]).astype(BF16)
    q = _dot(hn, wq_ref[...])
    heads = []
    for hd in range(XA_HEADS):
        sl = slice(hd * XA_DH, (hd + 1) * XA_DH)
        sc = lax.dot_general(q[:, sl].astype(BF16), k_ref[:, sl], (((1,), (1,)), ((), ())),
                             preferred_element_type=F32) * (XA_DH ** -0.5)
        e = jnp.exp(sc - jnp.max(sc, axis=-1, keepdims=True))
        p = e / jnp.sum(e, axis=-1, keepdims=True)
        heads.append(_dot(p.astype(BF16), v_ref[:, sl]))
    o = jnp.concatenate(heads, axis=1)
    h2 = h1 + _dot(o.astype(BF16), wo_ref[...])
    h2_ref[...] = h2
    if with_router:
        xf = _rms(h2, gffn_ref[...])
        xf_ref[...] = xf
        lg_ref[...] = jnp.dot(xf, wr_ref[...], preferred_element_type=F32, precision=lax.Precision.HIGHEST)


def _merge_xattn(h, gmix, yab, hz, w_g2, b_g2, w_mo, w_out, gxa, wq, kmem, vmem, wo, l,
                 router=None):
    bsz, s, _ = h.shape
    t = ROW_TILE
    m = kmem.shape[2]
    lsel3 = lambda b, i: (l, 0, 0)
    cur = lambda b, i: (b, i, 0)
    mem = lambda b, i: (l, b, 0, 0)
    sq = (None, D_MODEL, D_MODEL)
    in_specs = [
        pl.BlockSpec((None, t, D_MODEL), cur),
        _resident((None, 1, D_MODEL), lsel3),
        pl.BlockSpec((None, t, D_MODEL), cur),
        pl.BlockSpec((None, t, D_MODEL), cur),
        _resident(sq, lsel3),
        _resident((None, 1, D_MODEL), lsel3),
        _resident(sq, lsel3),
        _resident(sq, lsel3),
        _resident((None, 1, D_MODEL), lsel3),
        _resident(sq, lsel3),
        pl.BlockSpec((None, None, m, D_MODEL), mem),
        pl.BlockSpec((None, None, m, D_MODEL), mem),
        _resident(sq, lsel3),
    ]
    args = [h, gmix, yab, hz, w_g2, b_g2, w_mo, w_out, gxa, wq, kmem, vmem, wo]
    out_specs = [pl.BlockSpec((None, t, D_MODEL), cur)]
    out_shape = [jax.ShapeDtypeStruct((bsz, s, D_MODEL), F32)]
    if router is not None:
        gffn, w_router = router
        in_specs += [_resident((None, 1, D_MODEL), lsel3),
                     _resident((None, D_MODEL, LANES), lambda b, i: (0, 0, 0))]
        args += [gffn, w_router]
        out_specs += [pl.BlockSpec((None, t, D_MODEL), cur), pl.BlockSpec((None, t, LANES), cur)]
        out_shape += [jax.ShapeDtypeStruct((bsz, s, D_MODEL), F32), jax.ShapeDtypeStruct((bsz, s, LANES), F32)]
    return pl.pallas_call(
        functools.partial(_merge_xattn_body, router is not None),
        grid=(bsz, s // t),
        in_specs=in_specs,
        out_specs=out_specs,
        out_shape=out_shape,
        compiler_params=_params(2),
        name="merge_xattn_router" if router is not None else "merge_xattn",
    )(*args)


def _memkv_body(mem_ref, g_ref, wk_ref, wv_ref, k_ref, v_ref):
    mn = _rms(mem_ref[...], g_ref[...]).astype(BF16)
    k_ref[...] = _dot(mn, wk_ref[...]).astype(BF16)
    v_ref[...] = _dot(mn, wv_ref[...]).astype(BF16)


def _memkv(mem, g, wk, wv):
    bsz, m, _ = mem.shape
    depth = wk.shape[0]
    out = jax.ShapeDtypeStruct((depth, bsz, m, D_MODEL), BF16)
    return pl.pallas_call(
        _memkv_body,
        grid=(depth, bsz),
        in_specs=[
            pl.BlockSpec((None, m, D_MODEL), lambda l, b: (b, 0, 0)),
            pl.BlockSpec((1, D_MODEL), lambda l, b: (0, 0)),
            pl.BlockSpec((None, D_MODEL, D_MODEL), lambda l, b: (l, 0, 0)),
            pl.BlockSpec((None, D_MODEL, D_MODEL), lambda l, b: (l, 0, 0)),
        ],
        out_specs=[pl.BlockSpec((None, None, m, D_MODEL), lambda l, b: (l, b, 0, 0))] * 2,
        out_shape=[out, out],
        compiler_params=_params(2),
        name="memkv",
    )(mem, g, wk, wv)


def _ffn_body(final_norm, h_ref, g_ref, w1_ref, w3_ref, w2_ref, *rest):
    if final_norm:
        gfin_ref, out_ref = rest
    else:
        (out_ref,) = rest
    h = h_ref[...]
    xn = _rms(h, g_ref[...]).astype(BF16)
    acc = h
    ff = w1_ref.shape[1]
    for f0 in range(0, ff, FF_CHUNK_DENSE):
        a = _dot(xn, w1_ref[:, f0:f0 + FF_CHUNK_DENSE])
        b = _dot(xn, w3_ref[:, f0:f0 + FF_CHUNK_DENSE])
        acc = acc + _dot((a * _sigmoid(a) * b).astype(BF16), w2_ref[f0:f0 + FF_CHUNK_DENSE, :])
    out_ref[...] = _rms(acc, gfin_ref[...]) if final_norm else acc


def _ffn(h, gffn, w1, w3, w2, l, j, final_g=None):
    n = h.shape[0]
    t = ROW_TILE
    ff = w1.shape[2]
    row = lambda i: (i, 0)
    in_specs = [
        pl.BlockSpec((t, D_MODEL), row),
        _resident((None, 1, D_MODEL), lambda i: (l, 0, 0)),
        _resident((None, D_MODEL, ff), lambda i: (j, 0, 0)),
        _resident((None, D_MODEL, ff), lambda i: (j, 0, 0)),
        _resident((None, ff, D_MODEL), lambda i: (j, 0, 0)),
    ]
    args = [h, gffn, w1, w3, w2]
    if final_g is not None:
        in_specs.append(_resident((1, D_MODEL), lambda i: (0, 0)))
        args.append(final_g)
    return pl.pallas_call(
        functools.partial(_ffn_body, final_g is not None),
        grid=(n // t,),
        in_specs=in_specs,
        out_specs=pl.BlockSpec((t, D_MODEL), row),
        out_shape=jax.ShapeDtypeStruct((n, D_MODEL), F32),
        compiler_params=_params(1),
        name="ffn_dense",
    )(*args)


_R_E1, _R_E2, _R_G1, _R_G2, _R_RANK1, _R_RANK2 = range(6)


def _route_body(lg_ref, info_ref, cnt_ref, carry):
    t = lg_ref.shape[0]
    i = pl.program_id(0)

    @pl.when(i == 0)
    def _():
        carry[...] = jnp.zeros_like(carry)

    lane = lax.broadcasted_iota(jnp.int32, (t, LANES), 1)
    lanef = lane.astype(F32)
    lg = jnp.where(lane < N_EXPERTS, lg_ref[...], NEG)
    m1 = jnp.max(lg, axis=-1, keepdims=True)
    e1 = jnp.min(jnp.where(lg == m1, lanef, float(LANES)), axis=-1, keepdims=True)
    sel1 = lanef == e1
    lg2 = jnp.where(sel1, NEG, lg)
    m2 = jnp.max(lg2, axis=-1, keepdims=True)
    e2 = jnp.min(jnp.where(lg2 == m2, lanef, float(LANES)), axis=-1, keepdims=True)
    sel2 = lanef == e2
    ex = jnp.exp(m2 - m1)
    g1 = 1.0 / (1.0 + ex)
    g2 = ex / (1.0 + ex)

    onehot = jnp.where(sel1 | sel2, 1.0, 0.0)
    rr = lax.broadcasted_iota(jnp.int32, (t, t), 0)
    cc = lax.broadcasted_iota(jnp.int32, (t, t), 1)
    before = jnp.where(cc < rr, 1.0, 0.0).astype(BF16)
    pref = _dot(before, onehot.astype(BF16)) + carry[...]
    r1 = jnp.sum(jnp.where(sel1, pref, 0.0), axis=-1, keepdims=True)
    r2 = jnp.sum(jnp.where(sel2, pref, 0.0), axis=-1, keepdims=True)
    carry[...] = carry[...] + jnp.sum(onehot, axis=0, keepdims=True)

    info = jnp.zeros((t, LANES), F32)
    for col, val in ((_R_E1, e1), (_R_E2, e2), (_R_G1, g1), (_R_G2, g2), (_R_RANK1, r1), (_R_RANK2, r2)):
        info = jnp.where(lane == col, val, info)
    info_ref[...] = info
    cnt_ref[...] = carry[...]


def _route(logits):
    n = logits.shape[0]
    t = ROUTE_TILE
    return pl.pallas_call(
        _route_body,
        grid=(n // t,),
        in_specs=[pl.BlockSpec((t, LANES), lambda i: (i, 0))],
        out_specs=[pl.BlockSpec((t, LANES), lambda i: (i, 0)), pl.BlockSpec((1, LANES), lambda i: (0, 0))],
        out_shape=[jax.ShapeDtypeStruct((n, LANES), F32), jax.ShapeDtypeStruct((1, LANES), F32)],
        scratch_shapes=[pltpu.VMEM((1, LANES), F32)],
        compiler_params=_params(1),
        name="moe_route",
    )(logits)


def _row_copy_body(with_init, sidx_ref, didx_ref, src_ref, *rest):
    if with_init:
        _, dst_ref, sem = rest
    else:
        dst_ref, sem = rest
    i = pl.program_id(0)

    def row_dma(a):
        return pltpu.make_async_copy(src_ref.at[pl.ds(sidx_ref[a], 1)], dst_ref.at[pl.ds(didx_ref[a], 1)], sem)

    def start(r, carry):
        row_dma(i * COPY_ROWS + r).start()
        return carry

    def wait(r, carry):
        row_dma(i * COPY_ROWS + r).wait()
        return carry

    lax.fori_loop(0, COPY_ROWS, start, 0)
    lax.fori_loop(0, COPY_ROWS, wait, 0)


def _row_copy(src_idx, dst_idx, src, n_dst, dst_init=None):
    n_copies = src_idx.shape[0]
    width = src.shape[1]
    with_init = dst_init is not None
    any_spec = pl.BlockSpec(memory_space=pl.ANY)
    args = [src_idx, dst_idx, src] + ([dst_init] if with_init else [])
    return pl.pallas_call(
        functools.partial(_row_copy_body, with_init),
        grid_spec=pltpu.PrefetchScalarGridSpec(
            num_scalar_prefetch=2,
            grid=(n_copies // COPY_ROWS,),
            in_specs=[any_spec] * (2 if with_init else 1),
            out_specs=any_spec,
            scratch_shapes=[pltpu.SemaphoreType.DMA(())],
        ),
        out_shape=jax.ShapeDtypeStruct((n_dst, width), src.dtype),
        input_output_aliases={3: 0} if with_init else {},
        compiler_params=pltpu.CompilerParams(dimension_semantics=("arbitrary",), has_side_effects=True),
        name="row_copy",
    )(*args)


def _moe_gemm_body(be_ref, nv_ref, xs_ref, w1_ref, w3_ref, w2_ref, y_ref):
    b = pl.program_id(0)

    @pl.when(b < nv_ref[0])
    def _():
        x = xs_ref[...].astype(BF16)
        acc = jnp.zeros(y_ref.shape, F32)
        ff = w1_ref.shape[1]
        for f0 in range(0, ff, FF_CHUNK_MOE):
            a = _dot(x, w1_ref[:, f0:f0 + FF_CHUNK_MOE])
            g = _dot(x, w3_ref[:, f0:f0 + FF_CHUNK_MOE])
            acc = acc + _dot((a * _sigmoid(a) * g).astype(BF16), w2_ref[f0:f0 + FF_CHUNK_MOE, :])
        y_ref[...] = acc

    @pl.when(b >= nv_ref[0])
    def _():
        y_ref[...] = jnp.zeros_like(y_ref)


def _moe_gemm(block_e, n_valid, xs, w1, w3, w2, j):
    p = xs.shape[0]
    ff = w1.shape[3]
    return pl.pallas_call(
        _moe_gemm_body,
        grid_spec=pltpu.PrefetchScalarGridSpec(
            num_scalar_prefetch=2,
            grid=(p // MOE_BLOCK,),
            in_specs=[
                pl.BlockSpec((MOE_BLOCK, D_MODEL), lambda b, be, nv: (b, 0)),
                _resident((None, None, D_MODEL, ff), lambda b, be, nv: (j, be[b], 0, 0)),
                _resident((None, None, D_MODEL, ff), lambda b, be, nv: (j, be[b], 0, 0)),
                _resident((None, None, ff, D_MODEL), lambda b, be, nv: (j, be[b], 0, 0)),
            ],
            out_specs=pl.BlockSpec((MOE_BLOCK, D_MODEL), lambda b, be, nv: (b, 0)),
        ),
        out_shape=jax.ShapeDtypeStruct((p, D_MODEL), F32),
        compiler_params=_params(1),
        name="moe_gemm",
    )(block_e, n_valid, xs, w1, w3, w2)


def _combine_body(final_norm, h_ref, info_ref, y1_ref, y2_ref, *rest):
    if final_norm:
        gfin_ref, out_ref = rest
    else:
        (out_ref,) = rest
    info = info_ref[...]
    out = h_ref[...] + (info[:, _R_G1:_R_G1 + 1] * y1_ref[...] + info[:, _R_G2:_R_G2 + 1] * y2_ref[...])
    out_ref[...] = _rms(out, gfin_ref[...]) if final_norm else out


def _combine(h, info, ys, final_g=None):
    n = h.shape[0]
    t = ROW_TILE
    nt = n // t
    row = lambda i: (i, 0)
    in_specs = [
        pl.BlockSpec((t, D_MODEL), row),
        pl.BlockSpec((t, LANES), row),
        pl.BlockSpec((t, D_MODEL), row),
        pl.BlockSpec((t, D_MODEL), lambda i: (i + nt, 0)),
    ]
    args = [h, info, ys, ys]
    if final_g is not None:
        in_specs.append(_resident((1, D_MODEL), lambda i: (0, 0)))
        args.append(final_g)
    return pl.pallas_call(
        functools.partial(_combine_body, final_g is not None),
        grid=(nt,),
        in_specs=in_specs,
        out_specs=pl.BlockSpec((t, D_MODEL), row),
        out_shape=jax.ShapeDtypeStruct((n, D_MODEL), F32),
        compiler_params=_params(1),
        name="moe_combine",
    )(*args)


def _moe(h2, xf, logits, w1, w3, w2, j, final_g=None):
    n = h2.shape[0]
    info, counts = _route(logits)
    cnt = counts[0, :N_EXPERTS].astype(jnp.int32)
    padded = (cnt + MOE_BLOCK - 1) // MOE_BLOCK * MOE_BLOCK
    ends = jnp.cumsum(padded)
    starts = ends - padded
    n_blocks = (n * TOP_K) // MOE_BLOCK + N_EXPERTS
    block_e = jnp.minimum(jnp.searchsorted(ends, jnp.arange(n_blocks, dtype=jnp.int32) * MOE_BLOCK, side="right"),
                          N_EXPERTS - 1).astype(jnp.int32)
    n_valid = (ends[-1:] // MOE_BLOCK).astype(jnp.int32)
    e1 = info[:, _R_E1].astype(jnp.int32)
    e2 = info[:, _R_E2].astype(jnp.int32)
    slot = jnp.concatenate([starts[e1] + info[:, _R_RANK1].astype(jnp.int32),
                            starts[e2] + info[:, _R_RANK2].astype(jnp.int32)])
    tok = jnp.tile(jnp.arange(n, dtype=jnp.int32), TOP_K)
    p = n_blocks * MOE_BLOCK
    xs = _row_copy(tok, slot, xf, p, dst_init=jnp.zeros((p, D_MODEL), F32))
    yb = _moe_gemm(block_e, n_valid, xs, w1, w3, w2, j)
    ys = _row_copy(slot, jnp.arange(n * TOP_K, dtype=jnp.int32), yb, n * TOP_K)
    return _combine(h2, info, ys, final_g)


def _row3(a):
    return a.reshape(a.shape[0], 1, a.shape[1])


def kernel(x, mem, mem_norm_g, norm_mix_g, w_in, b_in, conv_dw_w, conv_dw_b, conv_ln_g, conv_ln_b, w_conv_out,
           pool_w, pool_scale, w_pool_out, mlstm_conv_w, mlstm_conv_b, mlstm_norm_g, w_mlstm_out, w_out,
           norm_xattn_g, xa_wq, xa_wk, xa_wv, xa_wo, norm_ffn_g, ffn_w1, ffn_w3, ffn_w2, router_w,
           moe_w1, moe_w3, moe_w2, final_norm_g):
    bsz, s, d = x.shape
    n = bsz * s
    depth = w_in.shape[0]
    bf = lambda a: a.astype(BF16)

    w_main = bf(w_in[:, :, :_M_IF])
    b_main = _row3(b_in[:, :_M_IF])
    pad_if = LANES - 2 * HEADS
    w_if = bf(jnp.pad(w_in[:, :, _M_IF:_G_PRE], ((0, 0), (0, 0), (0, pad_if))))
    b_if = _row3(jnp.pad(b_in[:, _M_IF:_G_PRE], ((0, 0), (0, pad_if))))
    w_g01 = bf(w_in[:, :, _G_PRE:_G_PRE + 2 * D_MODEL])
    b_g01 = _row3(b_in[:, _G_PRE:_G_PRE + 2 * D_MODEL])
    w_g2 = bf(w_in[:, :, _G_PRE + 2 * D_MODEL:])
    b_g2 = _row3(b_in[:, _G_PRE + 2 * D_MODEL:])
    gmix = _row3(norm_mix_g)
    gxa = _row3(norm_xattn_g)
    gffn = _row3(norm_ffn_g)
    final_g = final_norm_g.reshape(1, d)
    conv_b3, ln_g3, ln_b3 = _row3(conv_dw_b), _row3(conv_ln_g), _row3(conv_ln_b)
    w_co, w_po, pool_wb = bf(w_conv_out), bf(w_pool_out), bf(pool_w)
    pool_s3 = _row3(pool_scale)
    mconv_b3, mnorm_g3 = _row3(mlstm_conv_b), _row3(mlstm_norm_g)
    w_mo, w_o = bf(w_mlstm_out), bf(w_out)
    wq, wk, wv, wo = bf(xa_wq), bf(xa_wk), bf(xa_wv), bf(xa_wo)
    f1, f3, f2 = bf(ffn_w1), bf(ffn_w3), bf(ffn_w2)
    m1, m3, m2 = bf(moe_w1), bf(moe_w3), bf(moe_w2)
    w_router = jnp.pad(router_w, ((0, 0), (0, 0), (0, LANES - N_EXPERTS)))

    kmem, vmem = _memkv(mem, mem_norm_g.reshape(1, d), wk, wv)

    h = x
    for l in range(depth):
        last = l == depth - 1
        hflat = h.reshape(n, d)
        u, pu, qk, v, z, gif = _inproj(hflat, gmix, w_main, b_main, w_if, b_if, l)
        shp = lambda a: a.reshape(bsz, s, a.shape[-1])
        yab = _branches(h, gmix, shp(u), shp(pu), conv_dw_w, conv_b3, ln_g3, ln_b3, w_co, pool_wb, pool_s3,
                        w_po, w_g01, b_g01, l)
        hz = _mlstm(shp(qk), shp(v), shp(z), shp(gif), mlstm_conv_w, mconv_b3, mnorm_g3, l)
        j = l // 2
        if l % 2 == 0:
            (h2,) = _merge_xattn(h, gmix, yab, hz, w_g2, b_g2, w_mo, w_o, gxa, wq, kmem, vmem, wo, l)
            hnext = _ffn(h2.reshape(n, d), gffn, f1, f3, f2, l, j, final_g if last else None)
        else:
            h2, xf, logits = _merge_xattn(h, gmix, yab, hz, w_g2, b_g2, w_mo, w_o, gxa, wq, kmem, vmem, wo, l,
                                          router=(gffn, w_router[j:j + 1]))
            hnext = _moe(h2.reshape(n, d), xf.reshape(n, d), logits.reshape(n, LANES), m1, m3, m2, j,
                         final_g if last else None)
        h = hnext.reshape(bsz, s, d)
    return h
```

```python
import functools

import jax
import jax.numpy as jnp
from jax import lax
from jax.experimental import pallas as pl
from jax.experimental.pallas import tpu as pltpu

F32 = jnp.float32
BF16 = jnp.bfloat16
EPS = 1e-6
NEG = -1e30

D_MODEL = 1024
CONV_W = 512
CONV_K = 31
POOL_W = 512
POOL_GW = 128
POOL_WINDOWS = (2, 4, 8, 16)
MLSTM_W = 1024
HEADS = 4
DH = 256
MLSTM_CONV_K = 4
XA_HEADS = 4
XA_DH = 256
N_EXPERTS = 8
TOP_K = 2

LANES = 128
SUBLANES = 8
VMEM_LIMIT = 56 * 1024 * 1024

ROW_TILE = 512
CONV_HALO = 32
POOL_HALO = 16
QK_HALO = 8
VPU_ROWS = 64
MLSTM_CHUNK = 256
MOE_BLOCK = 512
FF_CHUNK_DENSE = 1408
FF_CHUNK_MOE = 896
ROUTE_TILE = 512
COPY_ROWS = 512

_C_A = 0
_C_B = _C_A + CONV_W
_P_U = _C_B + CONV_W
_M_QK = _P_U + POOL_W
_M_V = _M_QK + 2 * MLSTM_W
_M_Z = _M_V + MLSTM_W
_M_IF = _M_Z + MLSTM_W
_G_PRE = _M_IF + 2 * HEADS


def _params(n_axes=1):
    return pltpu.CompilerParams(dimension_semantics=("arbitrary",) * n_axes, vmem_limit_bytes=VMEM_LIMIT)


def _resident(block_shape, index_map):
    return pl.BlockSpec(block_shape, index_map, pipeline_mode=pl.Buffered(1))


def _rms(x, g):
    return x * lax.rsqrt(jnp.mean(x * x, axis=-1, keepdims=True) + EPS) * g


def _sigmoid(x):
    return 1.0 / (1.0 + jnp.exp(-x))


def _dot(a, b):
    return jnp.dot(a, b, preferred_element_type=F32)


def _inproj_body(h_ref, g_ref, w_ref, b_ref, wif_ref, bif_ref,
                 u_ref, pu_ref, qk_ref, v_ref, z_ref, if_ref):
    xn = _rms(h_ref[...], g_ref[...]).astype(BF16)

    def proj(lo, hi):
        return _dot(xn, w_ref[:, lo:hi]) + b_ref[:, lo:hi]

    u_ref[...] = proj(_C_A, _C_B) * _sigmoid(proj(_C_B, _P_U))
    pu_ref[...] = proj(_P_U, _M_QK)
    qk_ref[...] = proj(_M_QK, _M_V)
    v_ref[...] = proj(_M_V, _M_Z).astype(BF16)
    z_ref[...] = proj(_M_Z, _M_IF)
    if_ref[...] = _dot(xn, wif_ref[...]) + bif_ref[...]


def _inproj(h, gmix, w_main, b_main, w_if, b_if, l):
    n = h.shape[0]
    t = ROW_TILE
    lsel = lambda i: (l, 0, 0)
    row = lambda i: (i, 0)
    return pl.pallas_call(
        _inproj_body,
        grid=(n // t,),
        in_specs=[
            pl.BlockSpec((t, D_MODEL), row),
            _resident((None, 1, D_MODEL), lsel),
            _resident((None, D_MODEL, _M_IF), lsel),
            _resident((None, 1, _M_IF), lsel),
            _resident((None, D_MODEL, LANES), lsel),
            _resident((None, 1, LANES), lsel),
        ],
        out_specs=[
            pl.BlockSpec((t, CONV_W), row),
            pl.BlockSpec((t, POOL_W), row),
            pl.BlockSpec((t, 2 * MLSTM_W), row),
            pl.BlockSpec((t, MLSTM_W), row),
            pl.BlockSpec((t, MLSTM_W), row),
            pl.BlockSpec((t, LANES), row),
        ],
        out_shape=[
            jax.ShapeDtypeStruct((n, CONV_W), F32),
            jax.ShapeDtypeStruct((n, POOL_W), F32),
            jax.ShapeDtypeStruct((n, 2 * MLSTM_W), F32),
            jax.ShapeDtypeStruct((n, MLSTM_W), BF16),
            jax.ShapeDtypeStruct((n, MLSTM_W), F32),
            jax.ShapeDtypeStruct((n, LANES), F32),
        ],
        compiler_params=_params(1),
        name="inproj",
    )(h, gmix, w_main, b_main, w_if, b_if)


def _branches_body(h_ref, g_ref, u_ref, uprev_ref, pu_ref, puprev_ref,
                   cw_ref, cb_ref, lng_ref, lnb_ref, wco_ref,
                   pw_ref, ps_ref, wpo_ref, wg_ref, bg_ref,
                   yab_ref, extu, extp, convs, pools):
    t = u_ref.shape[0]
    i = pl.program_id(1)
    has_prev = i > 0

    extu[0:CONV_HALO, :] = jnp.where(has_prev, uprev_ref[...], 0.0)
    extu[CONV_HALO:CONV_HALO + t, :] = u_ref[...]
    base = CONV_HALO - (CONV_K - 1)
    for r0 in range(0, t, VPU_ROWS):
        for c0 in range(0, CONV_W, LANES):
            acc = jnp.broadcast_to(cb_ref[:, c0:c0 + LANES], (VPU_ROWS, LANES))
            for j in range(CONV_K):
                acc = acc + cw_ref[j:j + 1, c0:c0 + LANES] * extu[r0 + base + j:r0 + base + j + VPU_ROWS, c0:c0 + LANES]
            convs[r0:r0 + VPU_ROWS, c0:c0 + LANES] = acc
    cv = convs[...]
    mu = jnp.mean(cv, axis=-1, keepdims=True)
    var = jnp.mean(jnp.square(cv - mu), axis=-1, keepdims=True)
    a = (cv - mu) * lax.rsqrt(var + EPS) * lng_ref[...] + lnb_ref[...]
    a = a * _sigmoid(a)
    ya = _dot(a.astype(BF16), wco_ref[...])

    extp[0:POOL_HALO, :] = jnp.where(has_prev, puprev_ref[...], 0.0)
    extp[POOL_HALO:POOL_HALO + t, :] = pu_ref[...]
    for g, window in enumerate(POOL_WINDOWS):
        c0 = g * POOL_GW
        for r0 in range(0, t, VPU_ROWS):
            own = extp[POOL_HALO + r0:POOL_HALO + r0 + VPU_ROWS, c0:c0 + POOL_GW]
            acc = own
            for k in range(1, window):
                acc = acc + extp[POOL_HALO + r0 - k:POOL_HALO + r0 - k + VPU_ROWS, c0:c0 + POOL_GW]
            pos = i * t + r0 + lax.broadcasted_iota(jnp.int32, (VPU_ROWS, POOL_GW), 0)
            cnt = jnp.minimum(pos + 1, window).astype(F32)
            pools[r0:r0 + VPU_ROWS, c0:c0 + POOL_GW] = acc / cnt - own
    yp = jnp.concatenate(
        [_dot(pools[:, g * POOL_GW:(g + 1) * POOL_GW].astype(BF16), pw_ref[g]) for g in range(len(POOL_WINDOWS))],
        axis=1) * ps_ref[...]
    yb = _dot(yp.astype(BF16), wpo_ref[...])

    xn = _rms(h_ref[...], g_ref[...]).astype(BF16)
    gates = _sigmoid(_dot(xn, wg_ref[...]) + bg_ref[...])
    yab_ref[...] = gates[:, :D_MODEL] * ya + gates[:, D_MODEL:] * yb


def _branches(h, gmix, u, pu, conv_w, conv_b, ln_g, ln_b, w_conv_out, pool_w, pool_scale, w_pool_out,
              w_gate, b_gate, l):
    bsz, s, _ = h.shape
    t = ROW_TILE
    lsel3 = lambda b, i: (l, 0, 0)
    lsel4 = lambda b, i: (l, 0, 0, 0)
    cur = lambda b, i: (b, i, 0)
    prev_u = lambda b, i: (b, jnp.maximum(i * (t // CONV_HALO) - 1, 0), 0)
    prev_p = lambda b, i: (b, jnp.maximum(i * (t // POOL_HALO) - 1, 0), 0)
    return pl.pallas_call(
        _branches_body,
        grid=(bsz, s // t),
        in_specs=[
            pl.BlockSpec((None, t, D_MODEL), cur),
            _resident((None, 1, D_MODEL), lsel3),
            pl.BlockSpec((None, t, CONV_W), cur),
            pl.BlockSpec((None, CONV_HALO, CONV_W), prev_u),
            pl.BlockSpec((None, t, POOL_W), cur),
            pl.BlockSpec((None, POOL_HALO, POOL_W), prev_p),
            _resident((None, CONV_K, CONV_W), lsel3),
            _resident((None, 1, CONV_W), lsel3),
            _resident((None, 1, CONV_W), lsel3),
            _resident((None, 1, CONV_W), lsel3),
            _resident((None, CONV_W, D_MODEL), lsel3),
            _resident((None, len(POOL_WINDOWS), POOL_GW, POOL_GW), lsel4),
            _resident((None, 1, POOL_W), lsel3),
            _resident((None, POOL_W, D_MODEL), lsel3),
            _resident((None, D_MODEL, 2 * D_MODEL), lsel3),
            _resident((None, 1, 2 * D_MODEL), lsel3),
        ],
        out_specs=pl.BlockSpec((None, t, D_MODEL), cur),
        out_shape=jax.ShapeDtypeStruct((bsz, s, D_MODEL), F32),
        scratch_shapes=[
            pltpu.VMEM((CONV_HALO + t, CONV_W), F32),
            pltpu.VMEM((POOL_HALO + t, POOL_W), F32),
            pltpu.VMEM((t, CONV_W), F32),
            pltpu.VMEM((t, POOL_W), F32),
        ],
        compiler_params=_params(2),
        name="branches",
    )(h, gmix, u, u, pu, pu, conv_w, conv_b, ln_g, ln_b, w_conv_out, pool_w, pool_scale, w_pool_out,
      w_gate, b_gate)


def _cumsum_rows(x):
    n = x.shape[0]
    row = lax.broadcasted_iota(jnp.int32, x.shape, 0)
    shift = 1
    while shift < n:
        x = x + jnp.where(row >= shift, pltpu.roll(x, shift, axis=0), 0.0)
        shift *= 2
    return x


def _mlstm_body(qk_ref, qkprev_ref, v_ref, z_ref, if_ref, cw_ref, cb_ref, ng_ref,
                hz_ref, ext, ct_scr, n_scr, m_scr):
    ln = qk_ref.shape[0]
    c = pl.program_id(1)

    @pl.when(c == 0)
    def _():
        ct_scr[...] = jnp.zeros_like(ct_scr)
        n_scr[...] = jnp.zeros_like(n_scr)
        m_scr[...] = jnp.zeros_like(m_scr)

    ext[0:QK_HALO, :] = jnp.where(c > 0, qkprev_ref[...], 0.0)
    ext[QK_HALO:QK_HALO + ln, :] = qk_ref[...]
    base = QK_HALO - (MLSTM_CONV_K - 1)
    conv = jnp.broadcast_to(cb_ref[...], (ln, 2 * MLSTM_W))
    for j in range(MLSTM_CONV_K):
        conv = conv + cw_ref[j:j + 1, :] * ext[base + j:base + j + ln, :]
    qk = conv * _sigmoid(conv)

    gates = if_ref[...]
    logf = jnp.minimum(gates, 0.0) - jnp.log(1.0 + jnp.exp(-jnp.abs(gates)))
    bcum = _cumsum_rows(logf)
    lane = lax.broadcasted_iota(jnp.int32, gates.shape, 1)
    rows_t = jnp.where(lane < HEADS, gates, bcum).T

    tt = lax.broadcasted_iota(jnp.int32, (ln, ln), 0)
    ss = lax.broadcasted_iota(jnp.int32, (ln, ln), 1)
    causal = ss <= tt

    for hd in range(HEADS):
        q = qk[:, hd * DH:(hd + 1) * DH]
        k = qk[:, MLSTM_W + hd * DH:MLSTM_W + (hd + 1) * DH] * (DH ** -0.5)
        v = v_ref[:, hd * DH:(hd + 1) * DH]
        i_col = gates[:, hd:hd + 1]
        b_col = bcum[:, HEADS + hd:HEADS + hd + 1]
        i_row = rows_t[hd:hd + 1, :]
        b_row = rows_t[HEADS + hd:HEADS + hd + 1, :]
        m_prev = m_scr[hd][:, 0:1]
        ct = ct_scr[hd]
        nvec = n_scr[hd]

        log_d = jnp.where(causal, b_col - b_row + i_row, NEG)
        inter = b_col + m_prev
        mt = jnp.maximum(inter, jnp.max(log_d, axis=-1, keepdims=True))
        dmat = jnp.exp(log_d - mt)
        a_inter = jnp.exp(inter - mt)
        qb = q.astype(BF16)
        kb = k.astype(BF16)
        sc = lax.dot_general(qb, kb, (((1,), (1,)), ((), ())), preferred_element_type=F32) * dmat
        num = _dot(sc.astype(BF16), v) + a_inter * _dot(qb, ct.astype(BF16))
        den = jnp.sum(sc, axis=-1, keepdims=True) + a_inter * jnp.sum(q * nvec, axis=-1, keepdims=True)
        hh = num / jnp.maximum(jnp.abs(den), jnp.exp(-mt))

        b_last = b_col[ln - 1:ln, :]
        w_log = b_last - b_col + i_col
        m_new = jnp.maximum(b_last + m_prev, jnp.max(w_log, axis=0, keepdims=True))
        a_c = jnp.exp(b_last + m_prev - m_new)
        kw = k * jnp.exp(w_log - m_new)
        ct_scr[hd] = a_c * ct + lax.dot_general(kw.astype(BF16), v, (((0,), (0,)), ((), ())),
                                                preferred_element_type=F32)
        n_scr[hd] = a_c * nvec + jnp.sum(kw, axis=0, keepdims=True)
        m_scr[hd] = jnp.broadcast_to(m_new, (1, LANES))

        hn = hh * lax.rsqrt(jnp.mean(hh * hh, axis=-1, keepdims=True) + EPS) * ng_ref[:, hd * DH:(hd + 1) * DH]
        hz_ref[:, hd * DH:(hd + 1) * DH] = (hn * _sigmoid(z_ref[:, hd * DH:(hd + 1) * DH])).astype(BF16)


def _mlstm(qk, v, z, gates_if, conv_w, conv_b, norm_g, l):
    bsz, s, _ = qk.shape
    ln = MLSTM_CHUNK
    lsel3 = lambda b, c: (l, 0, 0)
    cur = lambda b, c: (b, c, 0)
    prev = lambda b, c: (b, jnp.maximum(c * (ln // QK_HALO) - 1, 0), 0)
    return pl.pallas_call(
        _mlstm_body,
        grid=(bsz, s // ln),
        in_specs=[
            pl.BlockSpec((None, ln, 2 * MLSTM_W), cur),
            pl.BlockSpec((None, QK_HALO, 2 * MLSTM_W), prev),
            pl.BlockSpec((None, ln, MLSTM_W), cur),
            pl.BlockSpec((None, ln, MLSTM_W), cur),
            pl.BlockSpec((None, ln, LANES), cur),
            _resident((None, MLSTM_CONV_K, 2 * MLSTM_W), lsel3),
            _resident((None, 1, 2 * MLSTM_W), lsel3),
            _resident((None, 1, MLSTM_W), lsel3),
        ],
        out_specs=pl.BlockSpec((None, ln, MLSTM_W), cur),
        out_shape=jax.ShapeDtypeStruct((bsz, s, MLSTM_W), BF16),
        scratch_shapes=[
            pltpu.VMEM((QK_HALO + ln, 2 * MLSTM_W), F32),
            pltpu.VMEM((HEADS, DH, DH), F32),
            pltpu.VMEM((HEADS, 1, DH), F32),
            pltpu.VMEM((HEADS, 1, LANES), F32),
        ],
        compiler_params=_params(2),
        name="mlstm",
    )(qk, qk, v, z, gates_if, conv_w, conv_b, norm_g)


def _merge_xattn_body(with_router, h_ref, gmix_ref, yab_ref, hz_ref, wg_ref, bg_ref, wmo_ref, wout_ref,
                      gxa_ref, wq_ref, k_ref, v_ref, wo_ref, *rest):
    if with_router:
        gffn_ref, wr_hi_ref, wr_lo_ref, h2_ref, xf_ref, lg_ref = rest
    else:
        (h2_ref,) = rest
    h = h_ref[...]
    xn = _rms(h, gmix_ref[...]).astype(BF16)
    g2 = _sigmoid(_dot(xn, wg_ref[...]) + bg_ref[...])
    y = yab_ref[...] + g2 * _dot(hz_ref[...], wmo_ref[...])
    h1 = h + _dot(y.astype(BF16), wout_ref[...])

    hn = _rms(h1, gxa_ref[...]).astype(BF16)
    q = _dot(hn, wq_ref[...])
    heads = []
    for hd in range(XA_HEADS):
        sl = slice(hd * XA_DH, (hd + 1) * XA_DH)
        sc = lax.dot_general(q[:, sl].astype(BF16), k_ref[:, sl], (((1,), (1,)), ((), ())),
                             preferred_element_type=F32) * (XA_DH ** -0.5)
        e = jnp.exp(sc - jnp.max(sc, axis=-1, keepdims=True))
        p = e / jnp.sum(e, axis=-1, keepdims=True)
        heads.append(_dot(p.astype(BF16), v_ref[:, sl]))
    o = jnp.concatenate(heads, axis=1)
    h2 = h1 + _dot(o.astype(BF16), wo_ref[...])
    h2_ref[...] = h2
    if with_router:
        xf = _rms(h2, gffn_ref[...])
        xf_ref[...] = xf
        xf_hi = xf.astype(BF16)
        xf_lo = (xf - xf_hi.astype(F32)).astype(BF16)
        lg_ref[...] = _dot(xf_hi, wr_hi_ref[...]) + (_dot(xf_lo, wr_hi_ref[...]) + _dot(xf_hi, wr_lo_ref[...]))


def _merge_xattn(h, gmix, yab, hz, w_g2, b_g2, w_mo, w_out, gxa, wq, kmem, vmem, wo, l,
                 router=None):
    bsz, s, _ = h.shape
    t = ROW_TILE
    m = kmem.shape[2]
    lsel3 = lambda b, i: (l, 0, 0)
    cur = lambda b, i: (b, i, 0)
    mem = lambda b, i: (l, b, 0, 0)
    sq = (None, D_MODEL, D_MODEL)
    in_specs = [
        pl.BlockSpec((None, t, D_MODEL), cur),
        _resident((None, 1, D_MODEL), lsel3),
        pl.BlockSpec((None, t, D_MODEL), cur),
        pl.BlockSpec((None, t, D_MODEL), cur),
        _resident(sq, lsel3),
        _resident((None, 1, D_MODEL), lsel3),
        _resident(sq, lsel3),
        _resident(sq, lsel3),
        _resident((None, 1, D_MODEL), lsel3),
        _resident(sq, lsel3),
        pl.BlockSpec((None, None, m, D_MODEL), mem),
        pl.BlockSpec((None, None, m, D_MODEL), mem),
        _resident(sq, lsel3),
    ]
    args = [h, gmix, yab, hz, w_g2, b_g2, w_mo, w_out, gxa, wq, kmem, vmem, wo]
    out_specs = [pl.BlockSpec((None, t, D_MODEL), cur)]
    out_shape = [jax.ShapeDtypeStruct((bsz, s, D_MODEL), F32)]
    if router is not None:
        gffn, w_router_hi, w_router_lo, j = router
        jsel = lambda b, i: (j, 0, 0)
        in_specs += [_resident((None, 1, D_MODEL), lsel3),
                     _resident((None, D_MODEL, LANES), jsel),
                     _resident((None, D_MODEL, LANES), jsel)]
        args += [gffn, w_router_hi, w_router_lo]
        out_specs += [pl.BlockSpec((None, t, D_MODEL), cur), pl.BlockSpec((None, t, LANES), cur)]
        out_shape += [jax.ShapeDtypeStruct((bsz, s, D_MODEL), F32), jax.ShapeDtypeStruct((bsz, s, LANES), F32)]
    return pl.pallas_call(
        functools.partial(_merge_xattn_body, router is not None),
        grid=(bsz, s // t),
        in_specs=in_specs,
        out_specs=out_specs,
        out_shape=out_shape,
        compiler_params=_params(2),
        name="merge_xattn_router" if router is not None else "merge_xattn",
    )(*args)


def _memkv_body(mem_ref, g_ref, wk_ref, wv_ref, k_ref, v_ref):
    mn = _rms(mem_ref[...], g_ref[...]).astype(BF16)
    k_ref[...] = _dot(mn, wk_ref[...]).astype(BF16)
    v_ref[...] = _dot(mn, wv_ref[...]).astype(BF16)


def _memkv(mem, g, wk, wv):
    bsz, m, _ = mem.shape
    depth = wk.shape[0]
    out = jax.ShapeDtypeStruct((depth, bsz, m, D_MODEL), BF16)
    return pl.pallas_call(
        _memkv_body,
        grid=(depth, bsz),
        in_specs=[
            pl.BlockSpec((None, m, D_MODEL), lambda l, b: (b, 0, 0)),
            pl.BlockSpec((1, D_MODEL), lambda l, b: (0, 0)),
            pl.BlockSpec((None, D_MODEL, D_MODEL), lambda l, b: (l, 0, 0)),
            pl.BlockSpec((None, D_MODEL, D_MODEL), lambda l, b: (l, 0, 0)),
        ],
        out_specs=[pl.BlockSpec((None, None, m, D_MODEL), lambda l, b: (l, b, 0, 0))] * 2,
        out_shape=[out, out],
        compiler_params=_params(2),
        name="memkv",
    )(mem, g, wk, wv)


def _ffn_body(final_norm, h_ref, g_ref, w1_ref, w3_ref, w2_ref, *rest):
    if final_norm:
        gfin_ref, out_ref = rest
    else:
        (out_ref,) = rest
    h = h_ref[...]
    xn = _rms(h, g_ref[...]).astype(BF16)
    acc = h
    ff = w1_ref.shape[1]
    for f0 in range(0, ff, FF_CHUNK_DENSE):
        a = _dot(xn, w1_ref[:, f0:f0 + FF_CHUNK_DENSE])
        b = _dot(xn, w3_ref[:, f0:f0 + FF_CHUNK_DENSE])
        acc = acc + _dot((a * _sigmoid(a) * b).astype(BF16), w2_ref[f0:f0 + FF_CHUNK_DENSE, :])
    out_ref[...] = _rms(acc, gfin_ref[...]) if final_norm else acc


def _ffn(h, gffn, w1, w3, w2, l, j, final_g=None):
    n = h.shape[0]
    t = ROW_TILE
    ff = w1.shape[2]
    row = lambda i: (i, 0)
    in_specs = [
        pl.BlockSpec((t, D_MODEL), row),
        _resident((None, 1, D_MODEL), lambda i: (l, 0, 0)),
        _resident((None, D_MODEL, ff), lambda i: (j, 0, 0)),
        _resident((None, D_MODEL, ff), lambda i: (j, 0, 0)),
        _resident((None, ff, D_MODEL), lambda i: (j, 0, 0)),
    ]
    args = [h, gffn, w1, w3, w2]
    if final_g is not None:
        in_specs.append(_resident((1, D_MODEL), lambda i: (0, 0)))
        args.append(final_g)
    return pl.pallas_call(
        functools.partial(_ffn_body, final_g is not None),
        grid=(n // t,),
        in_specs=in_specs,
        out_specs=pl.BlockSpec((t, D_MODEL), row),
        out_shape=jax.ShapeDtypeStruct((n, D_MODEL), F32),
        compiler_params=_params(1),
        name="ffn_dense",
    )(*args)


_R_E1, _R_E2, _R_G1, _R_G2, _R_RANK1, _R_RANK2 = range(6)


def _route_body(lg_ref, info_ref, cnt_ref, carry):
    t = lg_ref.shape[0]
    i = pl.program_id(0)

    @pl.when(i == 0)
    def _():
        carry[...] = jnp.zeros_like(carry)

    lane = lax.broadcasted_iota(jnp.int32, (t, LANES), 1)
    lanef = lane.astype(F32)
    lg = jnp.where(lane < N_EXPERTS, lg_ref[...], NEG)
    m1 = jnp.max(lg, axis=-1, keepdims=True)
    e1 = jnp.min(jnp.where(lg == m1, lanef, float(LANES)), axis=-1, keepdims=True)
    sel1 = lanef == e1
    lg2 = jnp.where(sel1, NEG, lg)
    m2 = jnp.max(lg2, axis=-1, keepdims=True)
    e2 = jnp.min(jnp.where(lg2 == m2, lanef, float(LANES)), axis=-1, keepdims=True)
    sel2 = lanef == e2
    ex = jnp.exp(m2 - m1)
    g1 = 1.0 / (1.0 + ex)
    g2 = ex / (1.0 + ex)

    onehot = jnp.where(sel1 | sel2, 1.0, 0.0)
    rr = lax.broadcasted_iota(jnp.int32, (t, t), 0)
    cc = lax.broadcasted_iota(jnp.int32, (t, t), 1)
    before = jnp.where(cc < rr, 1.0, 0.0).astype(BF16)
    pref = _dot(before, onehot.astype(BF16)) + carry[...]
    r1 = jnp.sum(jnp.where(sel1, pref, 0.0), axis=-1, keepdims=True)
    r2 = jnp.sum(jnp.where(sel2, pref, 0.0), axis=-1, keepdims=True)
    carry[...] = carry[...] + jnp.sum(onehot, axis=0, keepdims=True)

    info = jnp.zeros((t, LANES), F32)
    for col, val in ((_R_E1, e1), (_R_E2, e2), (_R_G1, g1), (_R_G2, g2), (_R_RANK1, r1), (_R_RANK2, r2)):
        info = jnp.where(lane == col, val, info)
    info_ref[...] = info
    cnt_ref[...] = carry[...]


def _route(logits):
    n = logits.shape[0]
    t = ROUTE_TILE
    return pl.pallas_call(
        _route_body,
        grid=(n // t,),
        in_specs=[pl.BlockSpec((t, LANES), lambda i: (i, 0))],
        out_specs=[pl.BlockSpec((t, LANES), lambda i: (i, 0)), pl.BlockSpec((1, LANES), lambda i: (0, 0))],
        out_shape=[jax.ShapeDtypeStruct((n, LANES), F32), jax.ShapeDtypeStruct((1, LANES), F32)],
        scratch_shapes=[pltpu.VMEM((1, LANES), F32)],
        compiler_params=_params(1),
        name="moe_route",
    )(logits)


def _row_copy_body(with_init, sidx_ref, didx_ref, src_ref, *rest):
    if with_init:
        _, dst_ref, sem = rest
    else:
        dst_ref, sem = rest
    i = pl.program_id(0)

    def row_dma(a):
        return pltpu.make_async_copy(src_ref.at[pl.ds(sidx_ref[a], 1)], dst_ref.at[pl.ds(didx_ref[a], 1)], sem)

    def start(r, carry):
        row_dma(i * COPY_ROWS + r).start()
        return carry

    def wait(r, carry):
        row_dma(i * COPY_ROWS + r).wait()
        return carry

    lax.fori_loop(0, COPY_ROWS, start, 0)
    lax.fori_loop(0, COPY_ROWS, wait, 0)


def _row_copy(src_idx, dst_idx, src, n_dst, dst_init=None):
    n_copies = src_idx.shape[0]
    width = src.shape[1]
    src = src.reshape(src.shape[0], 1, width)
    with_init = dst_init is not None
    if with_init:
        dst_init = dst_init.reshape(n_dst, 1, width)
    any_spec = pl.BlockSpec(memory_space=pl.ANY)
    args = [src_idx, dst_idx, src] + ([dst_init] if with_init else [])
    return pl.pallas_call(
        functools.partial(_row_copy_body, with_init),
        grid_spec=pltpu.PrefetchScalarGridSpec(
            num_scalar_prefetch=2,
            grid=(n_copies // COPY_ROWS,),
            in_specs=[any_spec] * (2 if with_init else 1),
            out_specs=any_spec,
            scratch_shapes=[pltpu.SemaphoreType.DMA(())],
        ),
        out_shape=jax.ShapeDtypeStruct((n_dst, 1, width), src.dtype),
        input_output_aliases={3: 0} if with_init else {},
        compiler_params=pltpu.CompilerParams(dimension_semantics=("arbitrary",), has_side_effects=True),
        name="row_copy",
    )(*args).reshape(n_dst, width)


def _moe_gemm_body(be_ref, nv_ref, xs_ref, w1_ref, w3_ref, w2_ref, y_ref):
    b = pl.program_id(0)

    @pl.when(b < nv_ref[0])
    def _():
        x = xs_ref[...].astype(BF16)
        acc = jnp.zeros(y_ref.shape, F32)
        ff = w1_ref.shape[1]
        for f0 in range(0, ff, FF_CHUNK_MOE):
            a = _dot(x, w1_ref[:, f0:f0 + FF_CHUNK_MOE])
            g = _dot(x, w3_ref[:, f0:f0 + FF_CHUNK_MOE])
            acc = acc + _dot((a * _sigmoid(a) * g).astype(BF16), w2_ref[f0:f0 + FF_CHUNK_MOE, :])
        y_ref[...] = acc

    @pl.when(b >= nv_ref[0])
    def _():
        y_ref[...] = jnp.zeros_like(y_ref)


def _moe_gemm(block_e, n_valid, xs, w1, w3, w2, j):
    p = xs.shape[0]
    ff = w1.shape[3]
    return pl.pallas_call(
        _moe_gemm_body,
        grid_spec=pltpu.PrefetchScalarGridSpec(
            num_scalar_prefetch=2,
            grid=(p // MOE_BLOCK,),
            in_specs=[
                pl.BlockSpec((MOE_BLOCK, D_MODEL), lambda b, be, nv: (b, 0)),
                _resident((None, None, D_MODEL, ff), lambda b, be, nv: (j, be[b], 0, 0)),
                _resident((None, None, D_MODEL, ff), lambda b, be, nv: (j, be[b], 0, 0)),
                _resident((None, None, ff, D_MODEL), lambda b, be, nv: (j, be[b], 0, 0)),
            ],
            out_specs=pl.BlockSpec((MOE_BLOCK, D_MODEL), lambda b, be, nv: (b, 0)),
        ),
        out_shape=jax.ShapeDtypeStruct((p, D_MODEL), F32),
        compiler_params=_params(1),
        name="moe_gemm",
    )(block_e, n_valid, xs, w1, w3, w2)


def _combine_body(final_norm, h_ref, info_ref, y1_ref, y2_ref, *rest):
    if final_norm:
        gfin_ref, out_ref = rest
    else:
        (out_ref,) = rest
    info = info_ref[...]
    out = h_ref[...] + (info[:, _R_G1:_R_G1 + 1] * y1_ref[...] + info[:, _R_G2:_R_G2 + 1] * y2_ref[...])
    out_ref[...] = _rms(out, gfin_ref[...]) if final_norm else out


def _combine(h, info, ys, final_g=None):
    n = h.shape[0]
    t = ROW_TILE
    nt = n // t
    row = lambda i: (i, 0)
    in_specs = [
        pl.BlockSpec((t, D_MODEL), row),
        pl.BlockSpec((t, LANES), row),
        pl.BlockSpec((t, D_MODEL), row),
        pl.BlockSpec((t, D_MODEL), lambda i: (i + nt, 0)),
    ]
    args = [h, info, ys, ys]
    if final_g is not None:
        in_specs.append(_resident((1, D_MODEL), lambda i: (0, 0)))
        args.append(final_g)
    return pl.pallas_call(
        functools.partial(_combine_body, final_g is not None),
        grid=(nt,),
        in_specs=in_specs,
        out_specs=pl.BlockSpec((t, D_MODEL), row),
        out_shape=jax.ShapeDtypeStruct((n, D_MODEL), F32),
        compiler_params=_params(1),
        name="moe_combine",
    )(*args)


def _moe(h2, xf, logits, w1, w3, w2, j, final_g=None):
    n = h2.shape[0]
    info, counts = _route(logits)
    cnt = counts[0, :N_EXPERTS].astype(jnp.int32)
    padded = (cnt + MOE_BLOCK - 1) // MOE_BLOCK * MOE_BLOCK
    ends = jnp.cumsum(padded)
    starts = ends - padded
    n_blocks = (n * TOP_K) // MOE_BLOCK + N_EXPERTS
    block_e = jnp.minimum(jnp.searchsorted(ends, jnp.arange(n_blocks, dtype=jnp.int32) * MOE_BLOCK, side="right"),
                          N_EXPERTS - 1).astype(jnp.int32)
    n_valid = (ends[-1:] // MOE_BLOCK).astype(jnp.int32)
    e1 = info[:, _R_E1].astype(jnp.int32)
    e2 = info[:, _R_E2].astype(jnp.int32)
    slot = jnp.concatenate([starts[e1] + info[:, _R_RANK1].astype(jnp.int32),
                            starts[e2] + info[:, _R_RANK2].astype(jnp.int32)])
    tok = jnp.tile(jnp.arange(n, dtype=jnp.int32), TOP_K)
    p = n_blocks * MOE_BLOCK
    xs = _row_copy(tok, slot, xf, p, dst_init=jnp.zeros((p, D_MODEL), F32))
    yb = _moe_gemm(block_e, n_valid, xs, w1, w3, w2, j)
    ys = _row_copy(slot, jnp.arange(n * TOP_K, dtype=jnp.int32), yb, n * TOP_K)
    return _combine(h2, info, ys, final_g)


def _row3(a):
    return a.reshape(a.shape[0], 1, a.shape[1])


def kernel(x, mem, mem_norm_g, norm_mix_g, w_in, b_in, conv_dw_w, conv_dw_b, conv_ln_g, conv_ln_b, w_conv_out,
           pool_w, pool_scale, w_pool_out, mlstm_conv_w, mlstm_conv_b, mlstm_norm_g, w_mlstm_out, w_out,
           norm_xattn_g, xa_wq, xa_wk, xa_wv, xa_wo, norm_ffn_g, ffn_w1, ffn_w3, ffn_w2, router_w,
           moe_w1, moe_w3, moe_w2, final_norm_g):
    bsz, s, d = x.shape
    n = bsz * s
    depth = w_in.shape[0]
    bf = lambda a: a.astype(BF16)

    w_main = bf(w_in[:, :, :_M_IF])
    b_main = _row3(b_in[:, :_M_IF])
    pad_if = LANES - 2 * HEADS
    w_if = bf(jnp.pad(w_in[:, :, _M_IF:_G_PRE], ((0, 0), (0, 0), (0, pad_if))))
    b_if = _row3(jnp.pad(b_in[:, _M_IF:_G_PRE], ((0, 0), (0, pad_if))))
    w_g01 = bf(w_in[:, :, _G_PRE:_G_PRE + 2 * D_MODEL])
    b_g01 = _row3(b_in[:, _G_PRE:_G_PRE + 2 * D_MODEL])
    w_g2 = bf(w_in[:, :, _G_PRE + 2 * D_MODEL:])
    b_g2 = _row3(b_in[:, _G_PRE + 2 * D_MODEL:])
    gmix = _row3(norm_mix_g)
    gxa = _row3(norm_xattn_g)
    gffn = _row3(norm_ffn_g)
    final_g = final_norm_g.reshape(1, d)
    conv_b3, ln_g3, ln_b3 = _row3(conv_dw_b), _row3(conv_ln_g), _row3(conv_ln_b)
    w_co, w_po, pool_wb = bf(w_conv_out), bf(w_pool_out), bf(pool_w)
    pool_s3 = _row3(pool_scale)
    mconv_b3, mnorm_g3 = _row3(mlstm_conv_b), _row3(mlstm_norm_g)
    w_mo, w_o = bf(w_mlstm_out), bf(w_out)
    wq, wk, wv, wo = bf(xa_wq), bf(xa_wk), bf(xa_wv), bf(xa_wo)
    f1, f3, f2 = bf(ffn_w1), bf(ffn_w3), bf(ffn_w2)
    m1, m3, m2 = bf(moe_w1), bf(moe_w3), bf(moe_w2)
    w_router = jnp.pad(router_w, ((0, 0), (0, 0), (0, LANES - N_EXPERTS)))
    w_router_hi = bf(w_router)
    w_router_lo = bf(w_router - w_router_hi.astype(F32))

    kmem, vmem = _memkv(mem, mem_norm_g.reshape(1, d), wk, wv)

    h = x
    for l in range(depth):
        last = l == depth - 1
        hflat = h.reshape(n, d)
        u, pu, qk, v, z, gif = _inproj(hflat, gmix, w_main, b_main, w_if, b_if, l)
        shp = lambda a: a.reshape(bsz, s, a.shape[-1])
        yab = _branches(h, gmix, shp(u), shp(pu), conv_dw_w, conv_b3, ln_g3, ln_b3, w_co, pool_wb, pool_s3,
                        w_po, w_g01, b_g01, l)
        hz = _mlstm(shp(qk), shp(v), shp(z), shp(gif), mlstm_conv_w, mconv_b3, mnorm_g3, l)
        j = l // 2
        if l % 2 == 0:
            (h2,) = _merge_xattn(h, gmix, yab, hz, w_g2, b_g2, w_mo, w_o, gxa, wq, kmem, vmem, wo, l)
            hnext = _ffn(h2.reshape(n, d), gffn, f1, f3, f2, l, j, final_g if last else None)
        else:
            h2, xf, logits = _merge_xattn(h, gmix, yab, hz, w_g2, b_g2, w_mo, w_o, gxa, wq, kmem, vmem, wo, l,
                                          router=(gffn, w_router_hi, w_router_lo, j))
            hnext = _moe(h2.reshape(n, d), xf.reshape(n, d), logits.reshape(n, LANES), m1, m3, m2, j,
                         final_g if last else None)
        h = hnext.reshape(bsz, s, d)
    return h
```

```python
import functools

import jax
import jax.numpy as jnp
from jax import lax
from jax.experimental import pallas as pl
from jax.experimental.pallas import tpu as pltpu

F32 = jnp.float32
BF16 = jnp.bfloat16
EPS = 1e-6
NEG = -1e30

D_MODEL = 1024
CONV_W = 512
CONV_K = 31
POOL_W = 512
POOL_GW = 128
POOL_WINDOWS = (2, 4, 8, 16)
MLSTM_W = 1024
HEADS = 4
DH = 256
MLSTM_CONV_K = 4
XA_HEADS = 4
XA_DH = 256
N_EXPERTS = 8
TOP_K = 2

LANES = 128
SUBLANES = 8
VMEM_LIMIT = 56 * 1024 * 1024

ROW_TILE = 512
CONV_HALO = 32
POOL_HALO = 16
QK_HALO = 8
VPU_ROWS = 64
MLSTM_CHUNK = 256
MOE_BLOCK = 512
FF_CHUNK_DENSE = 1408
FF_CHUNK_MOE = 896
ROUTE_TILE = 512
DMA_UNROLL = 8

_C_A = 0
_C_B = _C_A + CONV_W
_P_U = _C_B + CONV_W
_M_QK = _P_U + POOL_W
_M_V = _M_QK + 2 * MLSTM_W
_M_Z = _M_V + MLSTM_W
_M_IF = _M_Z + MLSTM_W
_G_PRE = _M_IF + 2 * HEADS


def _params(n_axes=1):
    return pltpu.CompilerParams(dimension_semantics=("arbitrary",) * n_axes, vmem_limit_bytes=VMEM_LIMIT)


def _resident(block_shape, index_map):
    return pl.BlockSpec(block_shape, index_map, pipeline_mode=pl.Buffered(1))


def _rms(x, g):
    return x * lax.rsqrt(jnp.mean(x * x, axis=-1, keepdims=True) + EPS) * g


def _sigmoid(x):
    return 1.0 / (1.0 + jnp.exp(-x))


def _dot(a, b):
    return jnp.dot(a, b, preferred_element_type=F32)


def _inproj_body(h_ref, g_ref, w_ref, b_ref, wif_ref, bif_ref,
                 u_ref, pu_ref, qk_ref, v_ref, z_ref, if_ref):
    xn = _rms(h_ref[...], g_ref[...]).astype(BF16)

    def proj(lo, hi):
        return _dot(xn, w_ref[:, lo:hi]) + b_ref[:, lo:hi]

    u_ref[...] = proj(_C_A, _C_B) * _sigmoid(proj(_C_B, _P_U))
    pu_ref[...] = proj(_P_U, _M_QK)
    qk_ref[...] = proj(_M_QK, _M_V)
    v_ref[...] = proj(_M_V, _M_Z).astype(BF16)
    z_ref[...] = proj(_M_Z, _M_IF)
    if_ref[...] = _dot(xn, wif_ref[...]) + bif_ref[...]


def _inproj(h, gmix, w_main, b_main, w_if, b_if, l):
    n = h.shape[0]
    t = ROW_TILE
    lsel = lambda i: (l, 0, 0)
    row = lambda i: (i, 0)
    return pl.pallas_call(
        _inproj_body,
        grid=(n // t,),
        in_specs=[
            pl.BlockSpec((t, D_MODEL), row),
            _resident((None, 1, D_MODEL), lsel),
            _resident((None, D_MODEL, _M_IF), lsel),
            _resident((None, 1, _M_IF), lsel),
            _resident((None, D_MODEL, LANES), lsel),
            _resident((None, 1, LANES), lsel),
        ],
        out_specs=[
            pl.BlockSpec((t, CONV_W), row),
            pl.BlockSpec((t, POOL_W), row),
            pl.BlockSpec((t, 2 * MLSTM_W), row),
            pl.BlockSpec((t, MLSTM_W), row),
            pl.BlockSpec((t, MLSTM_W), row),
            pl.BlockSpec((t, LANES), row),
        ],
        out_shape=[
            jax.ShapeDtypeStruct((n, CONV_W), F32),
            jax.ShapeDtypeStruct((n, POOL_W), F32),
            jax.ShapeDtypeStruct((n, 2 * MLSTM_W), F32),
            jax.ShapeDtypeStruct((n, MLSTM_W), BF16),
            jax.ShapeDtypeStruct((n, MLSTM_W), F32),
            jax.ShapeDtypeStruct((n, LANES), F32),
        ],
        compiler_params=_params(1),
        name="inproj",
    )(h, gmix, w_main, b_main, w_if, b_if)


def _branches_body(h_ref, g_ref, u_ref, uprev_ref, pu_ref, puprev_ref,
                   cw_ref, cb_ref, lng_ref, lnb_ref, wco_ref,
                   pw_ref, ps_ref, wpo_ref, wg_ref, bg_ref,
                   yab_ref, extu, extp, convs, pools):
    t = u_ref.shape[0]
    i = pl.program_id(1)
    has_prev = i > 0

    extu[0:CONV_HALO, :] = jnp.where(has_prev, uprev_ref[...], 0.0)
    extu[CONV_HALO:CONV_HALO + t, :] = u_ref[...]
    base = CONV_HALO - (CONV_K - 1)
    for r0 in range(0, t, VPU_ROWS):
        for c0 in range(0, CONV_W, LANES):
            acc = jnp.broadcast_to(cb_ref[:, c0:c0 + LANES], (VPU_ROWS, LANES))
            for phase in range(SUBLANES):
                taps = range(phase, CONV_K, SUBLANES)
                rows = VPU_ROWS + taps[-1] - phase
                window = extu[r0 + base + phase:r0 + base + phase + rows, c0:c0 + LANES]
                for j in taps:
                    acc = acc + cw_ref[j:j + 1, c0:c0 + LANES] * window[j - phase:j - phase + VPU_ROWS]
            convs[r0:r0 + VPU_ROWS, c0:c0 + LANES] = acc
    cv = convs[...]
    mu = jnp.mean(cv, axis=-1, keepdims=True)
    var = jnp.mean(jnp.square(cv - mu), axis=-1, keepdims=True)
    a = (cv - mu) * lax.rsqrt(var + EPS) * lng_ref[...] + lnb_ref[...]
    a = a * _sigmoid(a)
    ya = _dot(a.astype(BF16), wco_ref[...])

    extp[0:POOL_HALO, :] = jnp.where(has_prev, puprev_ref[...], 0.0)
    extp[POOL_HALO:POOL_HALO + t, :] = pu_ref[...]
    for g, window in enumerate(POOL_WINDOWS):
        c0 = g * POOL_GW
        for r0 in range(0, t, VPU_ROWS):
            own = extp[POOL_HALO + r0:POOL_HALO + r0 + VPU_ROWS, c0:c0 + POOL_GW]
            acc = own
            for k in range(1, window):
                acc = acc + extp[POOL_HALO + r0 - k:POOL_HALO + r0 - k + VPU_ROWS, c0:c0 + POOL_GW]
            pos = i * t + r0 + lax.broadcasted_iota(jnp.int32, (VPU_ROWS, POOL_GW), 0)
            cnt = jnp.minimum(pos + 1, window).astype(F32)
            pools[r0:r0 + VPU_ROWS, c0:c0 + POOL_GW] = acc / cnt - own
    yp = jnp.concatenate(
        [_dot(pools[:, g * POOL_GW:(g + 1) * POOL_GW].astype(BF16), pw_ref[g]) for g in range(len(POOL_WINDOWS))],
        axis=1) * ps_ref[...]
    yb = _dot(yp.astype(BF16), wpo_ref[...])

    xn = _rms(h_ref[...], g_ref[...]).astype(BF16)
    gates = _sigmoid(_dot(xn, wg_ref[...]) + bg_ref[...])
    yab_ref[...] = gates[:, :D_MODEL] * ya + gates[:, D_MODEL:] * yb


def _branches(h, gmix, u, pu, conv_w, conv_b, ln_g, ln_b, w_conv_out, pool_w, pool_scale, w_pool_out,
              w_gate, b_gate, l):
    bsz, s, _ = h.shape
    t = ROW_TILE
    lsel3 = lambda b, i: (l, 0, 0)
    lsel4 = lambda b, i: (l, 0, 0, 0)
    cur = lambda b, i: (b, i, 0)
    prev_u = lambda b, i: (b, jnp.maximum(i * (t // CONV_HALO) - 1, 0), 0)
    prev_p = lambda b, i: (b, jnp.maximum(i * (t // POOL_HALO) - 1, 0), 0)
    return pl.pallas_call(
        _branches_body,
        grid=(bsz, s // t),
        in_specs=[
            pl.BlockSpec((None, t, D_MODEL), cur),
            _resident((None, 1, D_MODEL), lsel3),
            pl.BlockSpec((None, t, CONV_W), cur),
            pl.BlockSpec((None, CONV_HALO, CONV_W), prev_u),
            pl.BlockSpec((None, t, POOL_W), cur),
            pl.BlockSpec((None, POOL_HALO, POOL_W), prev_p),
            _resident((None, CONV_K, CONV_W), lsel3),
            _resident((None, 1, CONV_W), lsel3),
            _resident((None, 1, CONV_W), lsel3),
            _resident((None, 1, CONV_W), lsel3),
            _resident((None, CONV_W, D_MODEL), lsel3),
            _resident((None, len(POOL_WINDOWS), POOL_GW, POOL_GW), lsel4),
            _resident((None, 1, POOL_W), lsel3),
            _resident((None, POOL_W, D_MODEL), lsel3),
            _resident((None, D_MODEL, 2 * D_MODEL), lsel3),
            _resident((None, 1, 2 * D_MODEL), lsel3),
        ],
        out_specs=pl.BlockSpec((None, t, D_MODEL), cur),
        out_shape=jax.ShapeDtypeStruct((bsz, s, D_MODEL), F32),
        scratch_shapes=[
            pltpu.VMEM((CONV_HALO + t, CONV_W), F32),
            pltpu.VMEM((POOL_HALO + t, POOL_W), F32),
            pltpu.VMEM((t, CONV_W), F32),
            pltpu.VMEM((t, POOL_W), F32),
        ],
        compiler_params=_params(2),
        name="branches",
    )(h, gmix, u, u, pu, pu, conv_w, conv_b, ln_g, ln_b, w_conv_out, pool_w, pool_scale, w_pool_out,
      w_gate, b_gate)


def _cumsum_rows(x):
    n = x.shape[0]
    row = lax.broadcasted_iota(jnp.int32, x.shape, 0)
    shift = 1
    while shift < n:
        x = x + jnp.where(row >= shift, pltpu.roll(x, shift, axis=0), 0.0)
        shift *= 2
    return x


def _mlstm_body(qk_ref, qkprev_ref, v_ref, z_ref, if_ref, cw_ref, cb_ref, ng_ref,
                hz_ref, ext, ct_scr, n_scr, m_scr):
    ln = qk_ref.shape[0]
    c = pl.program_id(1)

    @pl.when(c == 0)
    def _():
        ct_scr[...] = jnp.zeros_like(ct_scr)
        n_scr[...] = jnp.zeros_like(n_scr)
        m_scr[...] = jnp.zeros_like(m_scr)

    ext[0:QK_HALO, :] = jnp.where(c > 0, qkprev_ref[...], 0.0)
    ext[QK_HALO:QK_HALO + ln, :] = qk_ref[...]
    base = QK_HALO - (MLSTM_CONV_K - 1)
    conv = jnp.broadcast_to(cb_ref[...], (ln, 2 * MLSTM_W))
    for j in range(MLSTM_CONV_K):
        conv = conv + cw_ref[j:j + 1, :] * ext[base + j:base + j + ln, :]
    qk = conv * _sigmoid(conv)

    gates = if_ref[...]
    logf = jnp.minimum(gates, 0.0) - jnp.log(1.0 + jnp.exp(-jnp.abs(gates)))
    bcum = _cumsum_rows(logf)
    lane = lax.broadcasted_iota(jnp.int32, gates.shape, 1)
    rows_t = jnp.where(lane < HEADS, gates, bcum).T

    tt = lax.broadcasted_iota(jnp.int32, (ln, ln), 0)
    ss = lax.broadcasted_iota(jnp.int32, (ln, ln), 1)
    causal = ss <= tt

    for hd in range(HEADS):
        q = qk[:, hd * DH:(hd + 1) * DH]
        k = qk[:, MLSTM_W + hd * DH:MLSTM_W + (hd + 1) * DH] * (DH ** -0.5)
        v = v_ref[:, hd * DH:(hd + 1) * DH]
        i_col = gates[:, hd:hd + 1]
        b_col = bcum[:, HEADS + hd:HEADS + hd + 1]
        i_row = rows_t[hd:hd + 1, :]
        b_row = rows_t[HEADS + hd:HEADS + hd + 1, :]
        m_prev = m_scr[hd][:, 0:1]
        ct = ct_scr[hd]
        nvec = n_scr[hd]

        log_d = jnp.where(causal, b_col - b_row + i_row, NEG)
        inter = b_col + m_prev
        mt = jnp.maximum(inter, jnp.max(log_d, axis=-1, keepdims=True))
        dmat = jnp.exp(log_d - mt)
        a_inter = jnp.exp(inter - mt)
        qb = q.astype(BF16)
        kb = k.astype(BF16)
        sc = lax.dot_general(qb, kb, (((1,), (1,)), ((), ())), preferred_element_type=F32) * dmat
        num = _dot(sc.astype(BF16), v) + a_inter * _dot(qb, ct.astype(BF16))
        den = jnp.sum(sc, axis=-1, keepdims=True) + a_inter * jnp.sum(q * nvec, axis=-1, keepdims=True)
        hh = num / jnp.maximum(jnp.abs(den), jnp.exp(-mt))

        b_last = b_col[ln - 1:ln, :]
        w_log = b_last - b_col + i_col
        m_new = jnp.maximum(b_last + m_prev, jnp.max(w_log, axis=0, keepdims=True))
        a_c = jnp.exp(b_last + m_prev - m_new)
        kw = k * jnp.exp(w_log - m_new)
        ct_scr[hd] = a_c * ct + lax.dot_general(kw.astype(BF16), v, (((0,), (0,)), ((), ())),
                                                preferred_element_type=F32)
        n_scr[hd] = a_c * nvec + jnp.sum(kw, axis=0, keepdims=True)
        m_scr[hd] = jnp.broadcast_to(m_new, (1, LANES))

        hn = hh * lax.rsqrt(jnp.mean(hh * hh, axis=-1, keepdims=True) + EPS) * ng_ref[:, hd * DH:(hd + 1) * DH]
        hz_ref[:, hd * DH:(hd + 1) * DH] = (hn * _sigmoid(z_ref[:, hd * DH:(hd + 1) * DH])).astype(BF16)


def _mlstm(qk, v, z, gates_if, conv_w, conv_b, norm_g, l):
    bsz, s, _ = qk.shape
    ln = MLSTM_CHUNK
    lsel3 = lambda b, c: (l, 0, 0)
    cur = lambda b, c: (b, c, 0)
    prev = lambda b, c: (b, jnp.maximum(c * (ln // QK_HALO) - 1, 0), 0)
    return pl.pallas_call(
        _mlstm_body,
        grid=(bsz, s // ln),
        in_specs=[
            pl.BlockSpec((None, ln, 2 * MLSTM_W), cur),
            pl.BlockSpec((None, QK_HALO, 2 * MLSTM_W), prev),
            pl.BlockSpec((None, ln, MLSTM_W), cur),
            pl.BlockSpec((None, ln, MLSTM_W), cur),
            pl.BlockSpec((None, ln, LANES), cur),
            _resident((None, MLSTM_CONV_K, 2 * MLSTM_W), lsel3),
            _resident((None, 1, 2 * MLSTM_W), lsel3),
            _resident((None, 1, MLSTM_W), lsel3),
        ],
        out_specs=pl.BlockSpec((None, ln, MLSTM_W), cur),
        out_shape=jax.ShapeDtypeStruct((bsz, s, MLSTM_W), BF16),
        scratch_shapes=[
            pltpu.VMEM((QK_HALO + ln, 2 * MLSTM_W), F32),
            pltpu.VMEM((HEADS, DH, DH), F32),
            pltpu.VMEM((HEADS, 1, DH), F32),
            pltpu.VMEM((HEADS, 1, LANES), F32),
        ],
        compiler_params=_params(2),
        name="mlstm",
    )(qk, qk, v, z, gates_if, conv_w, conv_b, norm_g)


def _merge_xattn_body(with_router, h_ref, gmix_ref, yab_ref, hz_ref, wg_ref, bg_ref, wmo_ref, wout_ref,
                      gxa_ref, wq_ref, k_ref, v_ref, wo_ref, *rest):
    if with_router:
        gffn_ref, wr_hi_ref, wr_lo_ref, h2_ref, xf_ref, lg_ref = rest
    else:
        (h2_ref,) = rest
    h = h_ref[...]
    xn = _rms(h, gmix_ref[...]).astype(BF16)
    g2 = _sigmoid(_dot(xn, wg_ref[...]) + bg_ref[...])
    y = yab_ref[...] + g2 * _dot(hz_ref[...], wmo_ref[...])
    h1 = h + _dot(y.astype(BF16), wout_ref[...])

    hn = _rms(h1, gxa_ref[...]).astype(BF16)
    q = _dot(hn, wq_ref[...])
    heads = []
    for hd in range(XA_HEADS):
        sl = slice(hd * XA_DH, (hd + 1) * XA_DH)
        sc = lax.dot_general(q[:, sl].astype(BF16), k_ref[:, sl], (((1,), (1,)), ((), ())),
                             preferred_element_type=F32) * (XA_DH ** -0.5)
        e = jnp.exp(sc - jnp.max(sc, axis=-1, keepdims=True))
        p = e / jnp.sum(e, axis=-1, keepdims=True)
        heads.append(_dot(p.astype(BF16), v_ref[:, sl]))
    o = jnp.concatenate(heads, axis=1)
    h2 = h1 + _dot(o.astype(BF16), wo_ref[...])
    h2_ref[...] = h2
    if with_router:
        xf = _rms(h2, gffn_ref[...])
        xf_ref[...] = xf
        xf_hi = xf.astype(BF16)
        xf_lo = (xf - xf_hi.astype(F32)).astype(BF16)
        lg_ref[...] = _dot(xf_hi, wr_hi_ref[...]) + (_dot(xf_lo, wr_hi_ref[...]) + _dot(xf_hi, wr_lo_ref[...]))


def _merge_xattn(h, gmix, yab, hz, w_g2, b_g2, w_mo, w_out, gxa, wq, kmem, vmem, wo, l,
                 router=None):
    bsz, s, _ = h.shape
    t = ROW_TILE
    m = kmem.shape[2]
    lsel3 = lambda b, i: (l, 0, 0)
    cur = lambda b, i: (b, i, 0)
    mem = lambda b, i: (l, b, 0, 0)
    sq = (None, D_MODEL, D_MODEL)
    in_specs = [
        pl.BlockSpec((None, t, D_MODEL), cur),
        _resident((None, 1, D_MODEL), lsel3),
        pl.BlockSpec((None, t, D_MODEL), cur),
        pl.BlockSpec((None, t, D_MODEL), cur),
        _resident(sq, lsel3),
        _resident((None, 1, D_MODEL), lsel3),
        _resident(sq, lsel3),
        _resident(sq, lsel3),
        _resident((None, 1, D_MODEL), lsel3),
        _resident(sq, lsel3),
        pl.BlockSpec((None, None, m, D_MODEL), mem),
        pl.BlockSpec((None, None, m, D_MODEL), mem),
        _resident(sq, lsel3),
    ]
    args = [h, gmix, yab, hz, w_g2, b_g2, w_mo, w_out, gxa, wq, kmem, vmem, wo]
    out_specs = [pl.BlockSpec((None, t, D_MODEL), cur)]
    out_shape = [jax.ShapeDtypeStruct((bsz, s, D_MODEL), F32)]
    if router is not None:
        gffn, w_router_hi, w_router_lo, j = router
        jsel = lambda b, i: (j, 0, 0)
        in_specs += [_resident((None, 1, D_MODEL), lsel3),
                     _resident((None, D_MODEL, LANES), jsel),
                     _resident((None, D_MODEL, LANES), jsel)]
        args += [gffn, w_router_hi, w_router_lo]
        out_specs += [pl.BlockSpec((None, t, D_MODEL), cur), pl.BlockSpec((None, t, LANES), cur)]
        out_shape += [jax.ShapeDtypeStruct((bsz, s, D_MODEL), F32), jax.ShapeDtypeStruct((bsz, s, LANES), F32)]
    return pl.pallas_call(
        functools.partial(_merge_xattn_body, router is not None),
        grid=(bsz, s // t),
        in_specs=in_specs,
        out_specs=out_specs,
        out_shape=out_shape,
        compiler_params=_params(2),
        name="merge_xattn_router" if router is not None else "merge_xattn",
    )(*args)


def _memkv_body(mem_ref, g_ref, wk_ref, wv_ref, k_ref, v_ref):
    mn = _rms(mem_ref[...], g_ref[...]).astype(BF16)
    k_ref[...] = _dot(mn, wk_ref[...]).astype(BF16)
    v_ref[...] = _dot(mn, wv_ref[...]).astype(BF16)


def _memkv(mem, g, wk, wv):
    bsz, m, _ = mem.shape
    depth = wk.shape[0]
    out = jax.ShapeDtypeStruct((depth, bsz, m, D_MODEL), BF16)
    return pl.pallas_call(
        _memkv_body,
        grid=(depth, bsz),
        in_specs=[
            pl.BlockSpec((None, m, D_MODEL), lambda l, b: (b, 0, 0)),
            pl.BlockSpec((1, D_MODEL), lambda l, b: (0, 0)),
            pl.BlockSpec((None, D_MODEL, D_MODEL), lambda l, b: (l, 0, 0)),
            pl.BlockSpec((None, D_MODEL, D_MODEL), lambda l, b: (l, 0, 0)),
        ],
        out_specs=[pl.BlockSpec((None, None, m, D_MODEL), lambda l, b: (l, b, 0, 0))] * 2,
        out_shape=[out, out],
        compiler_params=_params(2),
        name="memkv",
    )(mem, g, wk, wv)


def _ffn_body(final_norm, h_ref, g_ref, w1_ref, w3_ref, w2_ref, *rest):
    if final_norm:
        gfin_ref, out_ref = rest
    else:
        (out_ref,) = rest
    h = h_ref[...]
    xn = _rms(h, g_ref[...]).astype(BF16)
    acc = h
    ff = w1_ref.shape[1]
    for f0 in range(0, ff, FF_CHUNK_DENSE):
        a = _dot(xn, w1_ref[:, f0:f0 + FF_CHUNK_DENSE])
        b = _dot(xn, w3_ref[:, f0:f0 + FF_CHUNK_DENSE])
        acc = acc + _dot((a * _sigmoid(a) * b).astype(BF16), w2_ref[f0:f0 + FF_CHUNK_DENSE, :])
    out_ref[...] = _rms(acc, gfin_ref[...]) if final_norm else acc


def _ffn(h, gffn, w1, w3, w2, l, j, final_g=None):
    n = h.shape[0]
    t = ROW_TILE
    ff = w1.shape[2]
    row = lambda i: (i, 0)
    in_specs = [
        pl.BlockSpec((t, D_MODEL), row),
        _resident((None, 1, D_MODEL), lambda i: (l, 0, 0)),
        _resident((None, D_MODEL, ff), lambda i: (j, 0, 0)),
        _resident((None, D_MODEL, ff), lambda i: (j, 0, 0)),
        _resident((None, ff, D_MODEL), lambda i: (j, 0, 0)),
    ]
    args = [h, gffn, w1, w3, w2]
    if final_g is not None:
        in_specs.append(_resident((1, D_MODEL), lambda i: (0, 0)))
        args.append(final_g)
    return pl.pallas_call(
        functools.partial(_ffn_body, final_g is not None),
        grid=(n // t,),
        in_specs=in_specs,
        out_specs=pl.BlockSpec((t, D_MODEL), row),
        out_shape=jax.ShapeDtypeStruct((n, D_MODEL), F32),
        compiler_params=_params(1),
        name="ffn_dense",
    )(*args)


_R_E1, _R_E2, _R_G1, _R_G2, _R_RANK1, _R_RANK2 = range(6)


def _route_body(lg_ref, info_ref, cnt_ref, carry):
    t = lg_ref.shape[0]
    i = pl.program_id(0)

    @pl.when(i == 0)
    def _():
        carry[...] = jnp.zeros_like(carry)

    lane = lax.broadcasted_iota(jnp.int32, (t, LANES), 1)
    lanef = lane.astype(F32)
    lg = jnp.where(lane < N_EXPERTS, lg_ref[...], NEG)
    m1 = jnp.max(lg, axis=-1, keepdims=True)
    e1 = jnp.min(jnp.where(lg == m1, lanef, float(LANES)), axis=-1, keepdims=True)
    sel1 = lanef == e1
    lg2 = jnp.where(sel1, NEG, lg)
    m2 = jnp.max(lg2, axis=-1, keepdims=True)
    e2 = jnp.min(jnp.where(lg2 == m2, lanef, float(LANES)), axis=-1, keepdims=True)
    sel2 = lanef == e2
    ex = jnp.exp(m2 - m1)
    g1 = 1.0 / (1.0 + ex)
    g2 = ex / (1.0 + ex)

    onehot = jnp.where(sel1 | sel2, 1.0, 0.0)
    rr = lax.broadcasted_iota(jnp.int32, (t, t), 0)
    cc = lax.broadcasted_iota(jnp.int32, (t, t), 1)
    before = jnp.where(cc < rr, 1.0, 0.0).astype(BF16)
    pref = _dot(before, onehot.astype(BF16)) + carry[...]
    r1 = jnp.sum(jnp.where(sel1, pref, 0.0), axis=-1, keepdims=True)
    r2 = jnp.sum(jnp.where(sel2, pref, 0.0), axis=-1, keepdims=True)
    carry[...] = carry[...] + jnp.sum(onehot, axis=0, keepdims=True)

    info = jnp.zeros((t, LANES), F32)
    for col, val in ((_R_E1, e1), (_R_E2, e2), (_R_G1, g1), (_R_G2, g2), (_R_RANK1, r1), (_R_RANK2, r2)):
        info = jnp.where(lane == col, val, info)
    info_ref[...] = info
    cnt_ref[...] = carry[...]


def _route(logits):
    n = logits.shape[0]
    t = ROUTE_TILE
    return pl.pallas_call(
        _route_body,
        grid=(n // t,),
        in_specs=[pl.BlockSpec((t, LANES), lambda i: (i, 0))],
        out_specs=[pl.BlockSpec((t, LANES), lambda i: (i, 0)), pl.BlockSpec((1, LANES), lambda i: (0, 0))],
        out_shape=[jax.ShapeDtypeStruct((n, LANES), F32), jax.ShapeDtypeStruct((1, LANES), F32)],
        scratch_shapes=[pltpu.VMEM((1, LANES), F32)],
        compiler_params=_params(1),
        name="moe_route",
    )(logits)


def _moe_ffn_body(be_ref, nv_ref, stok_ref, sdst_ref, xf_hbm, w1_ref, w3_ref, w2_ref, ys_hbm,
                  xbuf, ybuf, gsem, ssem):
    b = pl.program_id(0)
    last_blk = pl.num_programs(0) - 1
    nv = nv_ref[0]
    cur = b % 2
    nxt = 1 - cur

    def gather_row(blk, buf, r):
        return pltpu.make_async_copy(xf_hbm.at[pl.ds(stok_ref[blk * MOE_BLOCK + r], 1)],
                                     xbuf.at[buf, pl.ds(r, 1)], gsem.at[buf])

    def scatter_row(blk, buf, r):
        return pltpu.make_async_copy(ybuf.at[buf, pl.ds(r, 1)],
                                     ys_hbm.at[pl.ds(sdst_ref[(blk + 1) * MOE_BLOCK + r], 1)], ssem.at[buf])

    def rolled(fn):
        def body(r, carry):
            fn(r)
            return carry
        lax.fori_loop(0, MOE_BLOCK, body, 0, unroll=DMA_UNROLL)

    def inline(fn):
        for r in range(MOE_BLOCK):
            fn(r)

    @pl.when(b == 0)
    def _():
        rolled(lambda r: gather_row(0, 0, r).start())
        ybuf[1] = jnp.zeros((MOE_BLOCK, D_MODEL), F32)

    @pl.when(b < nv)
    def _():
        inline(lambda r: gather_row(b, cur, r).wait())
        x = xbuf[cur].astype(BF16)
        ahead = jnp.minimum(b + 1, last_blk)
        inline(lambda r: gather_row(ahead, nxt, r).start())
        inline(lambda r: scatter_row(b - 1, nxt, r).start())
        acc = jnp.zeros((MOE_BLOCK, D_MODEL), F32)
        ff = w1_ref.shape[1]
        for f0 in range(0, ff, FF_CHUNK_MOE):
            a = _dot(x, w1_ref[:, f0:f0 + FF_CHUNK_MOE])
            g = _dot(x, w3_ref[:, f0:f0 + FF_CHUNK_MOE])
            acc = acc + _dot((a * _sigmoid(a) * g).astype(BF16), w2_ref[f0:f0 + FF_CHUNK_MOE, :])

        @pl.when(b >= 1)
        def _():
            inline(lambda r: scatter_row(b - 2, cur, r).wait())
        ybuf[cur] = acc

        @pl.when(b == nv - 1)
        def _():
            rolled(lambda r: gather_row(ahead, nxt, r).wait())
            rolled(lambda r: scatter_row(b, cur, r).start())
            rolled(lambda r: scatter_row(b - 1, nxt, r).wait())
            rolled(lambda r: scatter_row(b, cur, r).wait())

    @pl.when(b >= nv)
    def _():
        ybuf[0] = jnp.zeros((MOE_BLOCK, D_MODEL), F32)
        rolled(lambda r: scatter_row(b, 0, r).start())
        rolled(lambda r: scatter_row(b, 0, r).wait())


def _moe_ffn(block_e, n_valid, slot_tok, slot_dst, xf, w1, w3, w2, j):
    p = slot_tok.shape[0]
    ff = w1.shape[3]
    spare = p + jnp.arange(MOE_BLOCK, dtype=jnp.int32)
    slot_dst = jnp.concatenate([spare, slot_dst])
    any_spec = pl.BlockSpec(memory_space=pl.ANY)
    wsel = lambda b, be, nv, st, sd: (j, be[b], 0, 0)
    return pl.pallas_call(
        _moe_ffn_body,
        grid_spec=pltpu.PrefetchScalarGridSpec(
            num_scalar_prefetch=4,
            grid=(p // MOE_BLOCK,),
            in_specs=[
                any_spec,
                _resident((None, None, D_MODEL, ff), wsel),
                _resident((None, None, D_MODEL, ff), wsel),
                _resident((None, None, ff, D_MODEL), wsel),
            ],
            out_specs=any_spec,
            scratch_shapes=[
                pltpu.VMEM((2, MOE_BLOCK, D_MODEL), F32),
                pltpu.VMEM((2, MOE_BLOCK, D_MODEL), F32),
                pltpu.SemaphoreType.DMA((2,)),
                pltpu.SemaphoreType.DMA((2,)),
            ],
        ),
        out_shape=jax.ShapeDtypeStruct((p + MOE_BLOCK, D_MODEL), F32),
        compiler_params=pltpu.CompilerParams(dimension_semantics=("arbitrary",), vmem_limit_bytes=VMEM_LIMIT,
                                             has_side_effects=True),
        name="moe_ffn",
    )(block_e, n_valid, slot_tok, slot_dst, xf, w1, w3, w2)


def _combine_body(final_norm, h_ref, info_ref, y1_ref, y2_ref, *rest):
    if final_norm:
        gfin_ref, out_ref = rest
    else:
        (out_ref,) = rest
    info = info_ref[...]
    out = h_ref[...] + (info[:, _R_G1:_R_G1 + 1] * y1_ref[...] + info[:, _R_G2:_R_G2 + 1] * y2_ref[...])
    out_ref[...] = _rms(out, gfin_ref[...]) if final_norm else out


def _combine(h, info, ys, final_g=None):
    n = h.shape[0]
    t = ROW_TILE
    nt = n // t
    row = lambda i: (i, 0)
    in_specs = [
        pl.BlockSpec((t, D_MODEL), row),
        pl.BlockSpec((t, LANES), row),
        pl.BlockSpec((t, D_MODEL), row),
        pl.BlockSpec((t, D_MODEL), lambda i: (i + nt, 0)),
    ]
    args = [h, info, ys, ys]
    if final_g is not None:
        in_specs.append(_resident((1, D_MODEL), lambda i: (0, 0)))
        args.append(final_g)
    return pl.pallas_call(
        functools.partial(_combine_body, final_g is not None),
        grid=(nt,),
        in_specs=in_specs,
        out_specs=pl.BlockSpec((t, D_MODEL), row),
        out_shape=jax.ShapeDtypeStruct((n, D_MODEL), F32),
        compiler_params=_params(1),
        name="moe_combine",
    )(*args)


def _moe(h2, xf, logits, w1, w3, w2, j, final_g=None):
    n = h2.shape[0]
    info, counts = _route(logits)
    cnt = counts[0, :N_EXPERTS].astype(jnp.int32)
    padded = (cnt + MOE_BLOCK - 1) // MOE_BLOCK * MOE_BLOCK
    ends = jnp.cumsum(padded)
    starts = ends - padded
    n_blocks = (n * TOP_K) // MOE_BLOCK + N_EXPERTS
    block_e = jnp.minimum(jnp.searchsorted(ends, jnp.arange(n_blocks, dtype=jnp.int32) * MOE_BLOCK, side="right"),
                          N_EXPERTS - 1).astype(jnp.int32)
    n_valid = (ends[-1:] // MOE_BLOCK).astype(jnp.int32)
    e1 = info[:, _R_E1].astype(jnp.int32)
    e2 = info[:, _R_E2].astype(jnp.int32)
    slot = jnp.concatenate([starts[e1] + info[:, _R_RANK1].astype(jnp.int32),
                            starts[e2] + info[:, _R_RANK2].astype(jnp.int32)])
    n_asg = n * TOP_K
    p = n_blocks * MOE_BLOCK
    asg_of_slot = jnp.full((p,), -1, jnp.int32).at[slot].set(jnp.arange(n_asg, dtype=jnp.int32))
    is_pad = asg_of_slot < 0
    slot_tok = jnp.where(is_pad, 0, asg_of_slot % n)
    slot_dst = jnp.where(is_pad, n_asg - 1 + jnp.cumsum(is_pad.astype(jnp.int32)), asg_of_slot)
    ys = _moe_ffn(block_e, n_valid, slot_tok, slot_dst, xf, w1, w3, w2, j)
    return _combine(h2, info, ys, final_g)


def _row3(a):
    return a.reshape(a.shape[0], 1, a.shape[1])


def kernel(x, mem, mem_norm_g, norm_mix_g, w_in, b_in, conv_dw_w, conv_dw_b, conv_ln_g, conv_ln_b, w_conv_out,
           pool_w, pool_scale, w_pool_out, mlstm_conv_w, mlstm_conv_b, mlstm_norm_g, w_mlstm_out, w_out,
           norm_xattn_g, xa_wq, xa_wk, xa_wv, xa_wo, norm_ffn_g, ffn_w1, ffn_w3, ffn_w2, router_w,
           moe_w1, moe_w3, moe_w2, final_norm_g):
    bsz, s, d = x.shape
    n = bsz * s
    depth = w_in.shape[0]
    bf = lambda a: a.astype(BF16)

    w_main = bf(w_in[:, :, :_M_IF])
    b_main = _row3(b_in[:, :_M_IF])
    pad_if = LANES - 2 * HEADS
    w_if = bf(jnp.pad(w_in[:, :, _M_IF:_G_PRE], ((0, 0), (0, 0), (0, pad_if))))
    b_if = _row3(jnp.pad(b_in[:, _M_IF:_G_PRE], ((0, 0), (0, pad_if))))
    w_g01 = bf(w_in[:, :, _G_PRE:_G_PRE + 2 * D_MODEL])
    b_g01 = _row3(b_in[:, _G_PRE:_G_PRE + 2 * D_MODEL])
    w_g2 = bf(w_in[:, :, _G_PRE + 2 * D_MODEL:])
    b_g2 = _row3(b_in[:, _G_PRE + 2 * D_MODEL:])
    gmix = _row3(norm_mix_g)
    gxa = _row3(norm_xattn_g)
    gffn = _row3(norm_ffn_g)
    final_g = final_norm_g.reshape(1, d)
    conv_b3, ln_g3, ln_b3 = _row3(conv_dw_b), _row3(conv_ln_g), _row3(conv_ln_b)
    w_co, w_po, pool_wb = bf(w_conv_out), bf(w_pool_out), bf(pool_w)
    pool_s3 = _row3(pool_scale)
    mconv_b3, mnorm_g3 = _row3(mlstm_conv_b), _row3(mlstm_norm_g)
    w_mo, w_o = bf(w_mlstm_out), bf(w_out)
    wq, wk, wv, wo = bf(xa_wq), bf(xa_wk), bf(xa_wv), bf(xa_wo)
    f1, f3, f2 = bf(ffn_w1), bf(ffn_w3), bf(ffn_w2)
    m1, m3, m2 = bf(moe_w1), bf(moe_w3), bf(moe_w2)
    w_router = jnp.pad(router_w, ((0, 0), (0, 0), (0, LANES - N_EXPERTS)))
    w_router_hi = bf(w_router)
    w_router_lo = bf(w_router - w_router_hi.astype(F32))

    kmem, vmem = _memkv(mem, mem_norm_g.reshape(1, d), wk, wv)

    h = x
    for l in range(depth):
        last = l == depth - 1
        hflat = h.reshape(n, d)
        u, pu, qk, v, z, gif = _inproj(hflat, gmix, w_main, b_main, w_if, b_if, l)
        shp = lambda a: a.reshape(bsz, s, a.shape[-1])
        yab = _branches(h, gmix, shp(u), shp(pu), conv_dw_w, conv_b3, ln_g3, ln_b3, w_co, pool_wb, pool_s3,
                        w_po, w_g01, b_g01, l)
        hz = _mlstm(shp(qk), shp(v), shp(z), shp(gif), mlstm_conv_w, mconv_b3, mnorm_g3, l)
        j = l // 2
        if l % 2 == 0:
            (h2,) = _merge_xattn(h, gmix, yab, hz, w_g2, b_g2, w_mo, w_o, gxa, wq, kmem, vmem, wo, l)
            hnext = _ffn(h2.reshape(n, d), gffn, f1, f3, f2, l, j, final_g if last else None)
        else:
            h2, xf, logits = _merge_xattn(h, gmix, yab, hz, w_g2, b_g2, w_mo, w_o, gxa, wq, kmem, vmem, wo, l,
                                          router=(gffn, w_router_hi, w_router_lo, j))
            hnext = _moe(h2.reshape(n, d), xf.reshape(n, d), logits.reshape(n, LANES), m1, m3, m2, j,
                         final_g if last else None)
        h = hnext.reshape(bsz, s, d)
    return h
```

```python
import functools

import jax
import jax.numpy as jnp
from jax import lax
from jax.experimental import pallas as pl
from jax.experimental.pallas import tpu as pltpu

F32 = jnp.float32
BF16 = jnp.bfloat16
EPS = 1e-6
NEG = -1e30

D_MODEL = 1024
CONV_W = 512
CONV_K = 31
POOL_W = 512
POOL_GW = 128
POOL_WINDOWS = (2, 4, 8, 16)
MLSTM_W = 1024
HEADS = 4
DH = 256
MLSTM_CONV_K = 4
XA_HEADS = 4
XA_DH = 256
N_EXPERTS = 8
TOP_K = 2

LANES = 128
SUBLANES = 8
VMEM_LIMIT = 56 * 1024 * 1024

ROW_TILE = 512
IN_HALO = 32
POOL_HALO = 16
QK_HALO = 16
VPU_ROWS = 64
MLSTM_CHUNK = 256
MOE_BLOCK = 512
FF_CHUNK_DENSE = 1408
FF_CHUNK_MOE = 896
ROUTE_TILE = 512
DMA_UNROLL = 8

_C_A = 0
_C_B = _C_A + CONV_W
_P_U = _C_B + CONV_W
_M_QK = _P_U + POOL_W
_M_V = _M_QK + 2 * MLSTM_W
_M_Z = _M_V + MLSTM_W
_M_IF = _M_Z + MLSTM_W
_G_PRE = _M_IF + 2 * HEADS


def _params(n_axes=1):
    return pltpu.CompilerParams(dimension_semantics=("arbitrary",) * n_axes, vmem_limit_bytes=VMEM_LIMIT)


def _resident(block_shape, index_map):
    return pl.BlockSpec(block_shape, index_map, pipeline_mode=pl.Buffered(1))


def _rms(x, g):
    return x * lax.rsqrt(jnp.mean(x * x, axis=-1, keepdims=True) + EPS) * g


def _sigmoid(x):
    return 1.0 / (1.0 + jnp.exp(-x))


def _dot(a, b):
    return jnp.dot(a, b, preferred_element_type=F32)


def _inproj_body(h_ref, hprev_ref, g_ref, w_ref, b_ref, wif_ref, bif_ref, cw_ref, cb_ref, mw_ref, mb_ref,
                 cv_ref, pu_ref, q_ref, k_ref, v_ref, z_ref, if_ref,
                 xn_ext, u_ext, shifted, qk_ext):
    t = h_ref.shape[0]
    has_prev = pl.program_id(1) > 0

    xn_ext[0:IN_HALO, :] = _rms(hprev_ref[...], g_ref[...]).astype(BF16)
    xn_ext[IN_HALO:IN_HALO + t, :] = _rms(h_ref[...], g_ref[...]).astype(BF16)

    def proj(x, lo, hi):
        return _dot(x, w_ref[:, lo:hi]) + b_ref[:, lo:hi]

    def causal_input(y, halo):
        row = lax.broadcasted_iota(jnp.int32, y.shape, 0)
        return jnp.where(jnp.logical_or(has_prev, row >= halo), y, 0.0)

    def conv31_lanes(c0):
        base = IN_HALO - (CONV_K - 1)
        for phase in range(SUBLANES):
            rows = t + (CONV_K - 1 - phase) // SUBLANES * SUBLANES
            shifted[c0 // LANES % 2, phase, 0:rows, :] = u_ext[base + phase:base + phase + rows, c0:c0 + LANES]
        for r0 in range(0, t, VPU_ROWS):
            acc = jnp.broadcast_to(cb_ref[:, c0:c0 + LANES], (VPU_ROWS, LANES))
            for j in range(CONV_K):
                phase, off = j % SUBLANES, j // SUBLANES * SUBLANES
                acc = acc + cw_ref[j:j + 1, c0:c0 + LANES] * shifted[c0 // LANES % 2, phase, r0 + off:r0 + off + VPU_ROWS, :]
            cv_ref[r0:r0 + VPU_ROWS, c0:c0 + LANES] = acc

    def qk_project(half):
        lo = _M_QK + half * MLSTM_W
        xq = xn_ext[IN_HALO - QK_HALO:IN_HALO + t, :]
        qk_ext[half] = causal_input(proj(xq, lo, lo + MLSTM_W), QK_HALO)

    def qk_conv(half, out_ref):
        base = QK_HALO - (MLSTM_CONV_K - 1)
        cols = slice(half * MLSTM_W, (half + 1) * MLSTM_W)
        conv = jnp.broadcast_to(mb_ref[:, cols], (t, MLSTM_W))
        for j in range(MLSTM_CONV_K):
            conv = conv + mw_ref[j:j + 1, cols] * qk_ext[half, base + j:base + j + t, :]
        out_ref[...] = (conv * _sigmoid(conv)).astype(out_ref.dtype)

    xe = xn_ext[...]
    xn = xn_ext[IN_HALO:IN_HALO + t, :]
    u_ext[...] = causal_input(proj(xe, _C_A, _C_B) * _sigmoid(proj(xe, _C_B, _P_U)), IN_HALO)
    qk_project(0)
    conv31_lanes(0 * LANES)
    qk_project(1)
    qk_conv(0, q_ref)
    pu_ref[...] = proj(xn, _P_U, _M_QK)
    conv31_lanes(1 * LANES)
    v_ref[...] = proj(xn, _M_V, _M_Z).astype(BF16)
    qk_conv(1, k_ref)
    z_ref[...] = proj(xn, _M_Z, _M_IF)
    conv31_lanes(2 * LANES)
    if_ref[...] = _dot(xn, wif_ref[...]) + bif_ref[...]
    conv31_lanes(3 * LANES)


def _inproj(h, gmix, w_main, b_main, w_if, b_if, conv_w, conv_b, mconv_w, mconv_b, l):
    bsz, s, _ = h.shape
    t = ROW_TILE
    lsel = lambda b, i: (l, 0, 0)
    cur = lambda b, i: (b, i, 0)
    prev = lambda b, i: (b, jnp.maximum(i * (t // IN_HALO) - 1, 0), 0)
    widths = (CONV_W, POOL_W, MLSTM_W, MLSTM_W, MLSTM_W, MLSTM_W, LANES)
    dtypes = (F32, F32, BF16, F32, BF16, F32, F32)
    return pl.pallas_call(
        _inproj_body,
        grid=(bsz, s // t),
        in_specs=[
            pl.BlockSpec((None, t, D_MODEL), cur),
            pl.BlockSpec((None, IN_HALO, D_MODEL), prev),
            _resident((None, 1, D_MODEL), lsel),
            _resident((None, D_MODEL, _M_IF), lsel),
            _resident((None, 1, _M_IF), lsel),
            _resident((None, D_MODEL, LANES), lsel),
            _resident((None, 1, LANES), lsel),
            _resident((None, CONV_K, CONV_W), lsel),
            _resident((None, 1, CONV_W), lsel),
            _resident((None, MLSTM_CONV_K, 2 * MLSTM_W), lsel),
            _resident((None, 1, 2 * MLSTM_W), lsel),
        ],
        out_specs=[pl.BlockSpec((None, t, w), cur) for w in widths],
        out_shape=[jax.ShapeDtypeStruct((bsz, s, w), dt) for w, dt in zip(widths, dtypes)],
        scratch_shapes=[
            pltpu.VMEM((IN_HALO + t, D_MODEL), BF16),
            pltpu.VMEM((IN_HALO + t, CONV_W), F32),
            pltpu.VMEM((2, SUBLANES, t + (CONV_K - 1) // SUBLANES * SUBLANES, LANES), F32),
            pltpu.VMEM((2, QK_HALO + t, MLSTM_W), F32),
        ],
        compiler_params=_params(2),
        name="inproj",
    )(h, h, gmix, w_main, b_main, w_if, b_if, conv_w, conv_b, mconv_w, mconv_b)


def _branches_body(h_ref, g_ref, cv_ref, pu_ref, puprev_ref,
                   lng_ref, lnb_ref, wco_ref,
                   pw_ref, ps_ref, wpo_ref, wg_ref, bg_ref,
                   yab_ref, extp, pools):
    t = pu_ref.shape[0]
    i = pl.program_id(1)
    has_prev = i > 0

    cv = cv_ref[...]
    mu = jnp.mean(cv, axis=-1, keepdims=True)
    var = jnp.mean(jnp.square(cv - mu), axis=-1, keepdims=True)
    a = (cv - mu) * lax.rsqrt(var + EPS) * lng_ref[...] + lnb_ref[...]
    a = a * _sigmoid(a)
    ya = _dot(a.astype(BF16), wco_ref[...])

    extp[0:POOL_HALO, :] = jnp.where(has_prev, puprev_ref[...], 0.0)
    extp[POOL_HALO:POOL_HALO + t, :] = pu_ref[...]
    for g, window in enumerate(POOL_WINDOWS):
        c0 = g * POOL_GW
        for r0 in range(0, t, VPU_ROWS):
            own = extp[POOL_HALO + r0:POOL_HALO + r0 + VPU_ROWS, c0:c0 + POOL_GW]
            acc = own
            for k in range(1, window):
                acc = acc + extp[POOL_HALO + r0 - k:POOL_HALO + r0 - k + VPU_ROWS, c0:c0 + POOL_GW]
            pos = i * t + r0 + lax.broadcasted_iota(jnp.int32, (VPU_ROWS, POOL_GW), 0)
            cnt = jnp.minimum(pos + 1, window).astype(F32)
            pools[r0:r0 + VPU_ROWS, c0:c0 + POOL_GW] = acc / cnt - own
    yp = jnp.concatenate(
        [_dot(pools[:, g * POOL_GW:(g + 1) * POOL_GW].astype(BF16), pw_ref[g]) for g in range(len(POOL_WINDOWS))],
        axis=1) * ps_ref[...]
    yb = _dot(yp.astype(BF16), wpo_ref[...])

    xn = _rms(h_ref[...], g_ref[...]).astype(BF16)
    gates = _sigmoid(_dot(xn, wg_ref[...]) + bg_ref[...])
    yab_ref[...] = gates[:, :D_MODEL] * ya + gates[:, D_MODEL:] * yb


def _branches(h, gmix, cv, pu, ln_g, ln_b, w_conv_out, pool_w, pool_scale, w_pool_out, w_gate, b_gate, l):
    bsz, s, _ = h.shape
    t = ROW_TILE
    lsel3 = lambda b, i: (l, 0, 0)
    lsel4 = lambda b, i: (l, 0, 0, 0)
    cur = lambda b, i: (b, i, 0)
    prev_p = lambda b, i: (b, jnp.maximum(i * (t // POOL_HALO) - 1, 0), 0)
    return pl.pallas_call(
        _branches_body,
        grid=(bsz, s // t),
        in_specs=[
            pl.BlockSpec((None, t, D_MODEL), cur),
            _resident((None, 1, D_MODEL), lsel3),
            pl.BlockSpec((None, t, CONV_W), cur),
            pl.BlockSpec((None, t, POOL_W), cur),
            pl.BlockSpec((None, POOL_HALO, POOL_W), prev_p),
            _resident((None, 1, CONV_W), lsel3),
            _resident((None, 1, CONV_W), lsel3),
            _resident((None, CONV_W, D_MODEL), lsel3),
            _resident((None, len(POOL_WINDOWS), POOL_GW, POOL_GW), lsel4),
            _resident((None, 1, POOL_W), lsel3),
            _resident((None, POOL_W, D_MODEL), lsel3),
            _resident((None, D_MODEL, 2 * D_MODEL), lsel3),
            _resident((None, 1, 2 * D_MODEL), lsel3),
        ],
        out_specs=pl.BlockSpec((None, t, D_MODEL), cur),
        out_shape=jax.ShapeDtypeStruct((bsz, s, D_MODEL), F32),
        scratch_shapes=[
            pltpu.VMEM((POOL_HALO + t, POOL_W), F32),
            pltpu.VMEM((t, POOL_W), F32),
        ],
        compiler_params=_params(2),
        name="branches",
    )(h, gmix, cv, pu, pu, ln_g, ln_b, w_conv_out, pool_w, pool_scale, w_pool_out, w_gate, b_gate)


def _cumsum_rows(x):
    n = x.shape[0]
    row = lax.broadcasted_iota(jnp.int32, x.shape, 0)
    shift = 1
    while shift < n:
        x = x + jnp.where(row >= shift, pltpu.roll(x, shift, axis=0), 0.0)
        shift *= 2
    return x


def _mlstm_body(q_ref, k_ref, v_ref, z_ref, if_ref, ng_ref, hz_ref, ct_scr, n_scr, m_scr):
    ln = q_ref.shape[0]
    c = pl.program_id(1)

    @pl.when(c == 0)
    def _():
        ct_scr[...] = jnp.zeros_like(ct_scr)
        n_scr[...] = jnp.zeros_like(n_scr)
        m_scr[...] = jnp.zeros_like(m_scr)

    gates = if_ref[...]
    logf = jnp.minimum(gates, 0.0) - jnp.log(1.0 + jnp.exp(-jnp.abs(gates)))
    bcum = _cumsum_rows(logf)
    lane = lax.broadcasted_iota(jnp.int32, gates.shape, 1)
    rows_t = jnp.where(lane < HEADS, gates, bcum).T

    tt = lax.broadcasted_iota(jnp.int32, (ln, ln), 0)
    ss = lax.broadcasted_iota(jnp.int32, (ln, ln), 1)
    causal = ss <= tt

    for hd in range(HEADS):
        qb = q_ref[:, hd * DH:(hd + 1) * DH]
        q = qb.astype(F32)
        k = k_ref[:, hd * DH:(hd + 1) * DH] * (DH ** -0.5)
        v = v_ref[:, hd * DH:(hd + 1) * DH]
        i_col = gates[:, hd:hd + 1]
        b_col = bcum[:, HEADS + hd:HEADS + hd + 1]
        i_row = rows_t[hd:hd + 1, :]
        b_row = rows_t[HEADS + hd:HEADS + hd + 1, :]
        m_prev = m_scr[hd][:, 0:1]
        ct = ct_scr[hd]
        nvec = n_scr[hd]

        log_d = jnp.where(causal, b_col - b_row + i_row, NEG)
        inter = b_col + m_prev
        mt = jnp.maximum(inter, jnp.max(log_d, axis=-1, keepdims=True))
        dmat = jnp.exp(log_d - mt)
        a_inter = jnp.exp(inter - mt)
        kb = k.astype(BF16)
        sc = lax.dot_general(qb, kb, (((1,), (1,)), ((), ())), preferred_element_type=F32) * dmat
        num = _dot(sc.astype(BF16), v) + a_inter * _dot(qb, ct.astype(BF16))
        den = jnp.sum(sc, axis=-1, keepdims=True) + a_inter * jnp.sum(q * nvec, axis=-1, keepdims=True)
        hh = num / jnp.maximum(jnp.abs(den), jnp.exp(-mt))

        b_last = b_col[ln - 1:ln, :]
        w_log = b_last - b_col + i_col
        m_new = jnp.maximum(b_last + m_prev, jnp.max(w_log, axis=0, keepdims=True))
        a_c = jnp.exp(b_last + m_prev - m_new)
        kw = k * jnp.exp(w_log - m_new)
        ct_scr[hd] = a_c * ct + lax.dot_general(kw.astype(BF16), v, (((0,), (0,)), ((), ())),
                                                preferred_element_type=F32)
        n_scr[hd] = a_c * nvec + jnp.sum(kw, axis=0, keepdims=True)
        m_scr[hd] = jnp.broadcast_to(m_new, (1, LANES))

        hn = hh * lax.rsqrt(jnp.mean(hh * hh, axis=-1, keepdims=True) + EPS) * ng_ref[:, hd * DH:(hd + 1) * DH]
        hz_ref[:, hd * DH:(hd + 1) * DH] = (hn * _sigmoid(z_ref[:, hd * DH:(hd + 1) * DH])).astype(BF16)


def _mlstm(q, k, v, z, gates_if, norm_g, l):
    bsz, s, _ = q.shape
    ln = MLSTM_CHUNK
    lsel3 = lambda b, c: (l, 0, 0)
    cur = lambda b, c: (b, c, 0)
    wide = pl.BlockSpec((None, ln, MLSTM_W), cur)
    return pl.pallas_call(
        _mlstm_body,
        grid=(bsz, s // ln),
        in_specs=[wide, wide, wide, wide, pl.BlockSpec((None, ln, LANES), cur), _resident((None, 1, MLSTM_W), lsel3)],
        out_specs=wide,
        out_shape=jax.ShapeDtypeStruct((bsz, s, MLSTM_W), BF16),
        scratch_shapes=[
            pltpu.VMEM((HEADS, DH, DH), F32),
            pltpu.VMEM((HEADS, 1, DH), F32),
            pltpu.VMEM((HEADS, 1, LANES), F32),
        ],
        compiler_params=_params(2),
        name="mlstm",
    )(q, k, v, z, gates_if, norm_g)


def _merge_xattn_body(with_router, h_ref, gmix_ref, yab_ref, hz_ref, wg_ref, bg_ref, wmo_ref, wout_ref,
                      gxa_ref, wq_ref, k_ref, v_ref, wo_ref, *rest):
    if with_router:
        gffn_ref, wr_hi_ref, wr_lo_ref, h2_ref, xf_ref, lg_ref = rest
    else:
        (h2_ref,) = rest
    h = h_ref[...]
    xn = _rms(h, gmix_ref[...]).astype(BF16)
    g2 = _sigmoid(_dot(xn, wg_ref[...]) + bg_ref[...])
    y = yab_ref[...] + g2 * _dot(hz_ref[...], wmo_ref[...])
    h1 = h + _dot(y.astype(BF16), wout_ref[...])

    hn = _rms(h1, gxa_ref[...]).astype(BF16)
    q = _dot(hn, wq_ref[...])
    heads = []
    for hd in range(XA_HEADS):
        sl = slice(hd * XA_DH, (hd + 1) * XA_DH)
        sc = lax.dot_general(q[:, sl].astype(BF16), k_ref[:, sl], (((1,), (1,)), ((), ())),
                             preferred_element_type=F32) * (XA_DH ** -0.5)
        e = jnp.exp(sc - jnp.max(sc, axis=-1, keepdims=True))
        p = e / jnp.sum(e, axis=-1, keepdims=True)
        heads.append(_dot(p.astype(BF16), v_ref[:, sl]))
    o = jnp.concatenate(heads, axis=1)
    h2 = h1 + _dot(o.astype(BF16), wo_ref[...])
    h2_ref[...] = h2
    if with_router:
        xf = _rms(h2, gffn_ref[...])
        xf_ref[...] = xf
        xf_hi = xf.astype(BF16)
        xf_lo = (xf - xf_hi.astype(F32)).astype(BF16)
        lg_ref[...] = _dot(xf_hi, wr_hi_ref[...]) + (_dot(xf_lo, wr_hi_ref[...]) + _dot(xf_hi, wr_lo_ref[...]))


def _merge_xattn(h, gmix, yab, hz, w_g2, b_g2, w_mo, w_out, gxa, wq, kmem, vmem, wo, l,
                 router=None):
    bsz, s, _ = h.shape
    t = ROW_TILE
    m = kmem.shape[2]
    lsel3 = lambda b, i: (l, 0, 0)
    cur = lambda b, i: (b, i, 0)
    mem = lambda b, i: (l, b, 0, 0)
    sq = (None, D_MODEL, D_MODEL)
    in_specs = [
        pl.BlockSpec((None, t, D_MODEL), cur),
        _resident((None, 1, D_MODEL), lsel3),
        pl.BlockSpec((None, t, D_MODEL), cur),
        pl.BlockSpec((None, t, D_MODEL), cur),
        _resident(sq, lsel3),
        _resident((None, 1, D_MODEL), lsel3),
        _resident(sq, lsel3),
        _resident(sq, lsel3),
        _resident((None, 1, D_MODEL), lsel3),
        _resident(sq, lsel3),
        pl.BlockSpec((None, None, m, D_MODEL), mem),
        pl.BlockSpec((None, None, m, D_MODEL), mem),
        _resident(sq, lsel3),
    ]
    args = [h, gmix, yab, hz, w_g2, b_g2, w_mo, w_out, gxa, wq, kmem, vmem, wo]
    out_specs = [pl.BlockSpec((None, t, D_MODEL), cur)]
    out_shape = [jax.ShapeDtypeStruct((bsz, s, D_MODEL), F32)]
    if router is not None:
        gffn, w_router_hi, w_router_lo, j = router
        jsel = lambda b, i: (j, 0, 0)
        in_specs += [_resident((None, 1, D_MODEL), lsel3),
                     _resident((None, D_MODEL, LANES), jsel),
                     _resident((None, D_MODEL, LANES), jsel)]
        args += [gffn, w_router_hi, w_router_lo]
        out_specs += [pl.BlockSpec((None, t, D_MODEL), cur), pl.BlockSpec((None, t, LANES), cur)]
        out_shape += [jax.ShapeDtypeStruct((bsz, s, D_MODEL), F32), jax.ShapeDtypeStruct((bsz, s, LANES), F32)]
    return pl.pallas_call(
        functools.partial(_merge_xattn_body, router is not None),
        grid=(bsz, s // t),
        in_specs=in_specs,
        out_specs=out_specs,
        out_shape=out_shape,
        compiler_params=_params(2),
        name="merge_xattn_router" if router is not None else "merge_xattn",
    )(*args)


def _memkv_body(mem_ref, g_ref, wk_ref, wv_ref, k_ref, v_ref):
    mn = _rms(mem_ref[...], g_ref[...]).astype(BF16)
    k_ref[...] = _dot(mn, wk_ref[...]).astype(BF16)
    v_ref[...] = _dot(mn, wv_ref[...]).astype(BF16)


def _memkv(mem, g, wk, wv):
    bsz, m, _ = mem.shape
    depth = wk.shape[0]
    out = jax.ShapeDtypeStruct((depth, bsz, m, D_MODEL), BF16)
    return pl.pallas_call(
        _memkv_body,
        grid=(depth, bsz),
        in_specs=[
            pl.BlockSpec((None, m, D_MODEL), lambda l, b: (b, 0, 0)),
            pl.BlockSpec((1, D_MODEL), lambda l, b: (0, 0)),
            pl.BlockSpec((None, D_MODEL, D_MODEL), lambda l, b: (l, 0, 0)),
            pl.BlockSpec((None, D_MODEL, D_MODEL), lambda l, b: (l, 0, 0)),
        ],
        out_specs=[pl.BlockSpec((None, None, m, D_MODEL), lambda l, b: (l, b, 0, 0))] * 2,
        out_shape=[out, out],
        compiler_params=_params(2),
        name="memkv",
    )(mem, g, wk, wv)


def _ffn_body(final_norm, h_ref, g_ref, w1_ref, w3_ref, w2_ref, *rest):
    if final_norm:
        gfin_ref, out_ref = rest
    else:
        (out_ref,) = rest
    h = h_ref[...]
    xn = _rms(h, g_ref[...]).astype(BF16)
    acc = h
    ff = w1_ref.shape[1]
    for f0 in range(0, ff, FF_CHUNK_DENSE):
        a = _dot(xn, w1_ref[:, f0:f0 + FF_CHUNK_DENSE])
        b = _dot(xn, w3_ref[:, f0:f0 + FF_CHUNK_DENSE])
        acc = acc + _dot((a * _sigmoid(a) * b).astype(BF16), w2_ref[f0:f0 + FF_CHUNK_DENSE, :])
    out_ref[...] = _rms(acc, gfin_ref[...]) if final_norm else acc


def _ffn(h, gffn, w1, w3, w2, l, j, final_g=None):
    n = h.shape[0]
    t = ROW_TILE
    ff = w1.shape[2]
    row = lambda i: (i, 0)
    in_specs = [
        pl.BlockSpec((t, D_MODEL), row),
        _resident((None, 1, D_MODEL), lambda i: (l, 0, 0)),
        _resident((None, D_MODEL, ff), lambda i: (j, 0, 0)),
        _resident((None, D_MODEL, ff), lambda i: (j, 0, 0)),
        _resident((None, ff, D_MODEL), lambda i: (j, 0, 0)),
    ]
    args = [h, gffn, w1, w3, w2]
    if final_g is not None:
        in_specs.append(_resident((1, D_MODEL), lambda i: (0, 0)))
        args.append(final_g)
    return pl.pallas_call(
        functools.partial(_ffn_body, final_g is not None),
        grid=(n // t,),
        in_specs=in_specs,
        out_specs=pl.BlockSpec((t, D_MODEL), row),
        out_shape=jax.ShapeDtypeStruct((n, D_MODEL), F32),
        compiler_params=_params(1),
        name="ffn_dense",
    )(*args)


_R_E1, _R_E2, _R_G1, _R_G2, _R_RANK1, _R_RANK2 = range(6)


def _route_body(lg_ref, info_ref, cnt_ref, carry):
    t = lg_ref.shape[0]
    i = pl.program_id(0)

    @pl.when(i == 0)
    def _():
        carry[...] = jnp.zeros_like(carry)

    lane = lax.broadcasted_iota(jnp.int32, (t, LANES), 1)
    lanef = lane.astype(F32)
    lg = jnp.where(lane < N_EXPERTS, lg_ref[...], NEG)
    m1 = jnp.max(lg, axis=-1, keepdims=True)
    e1 = jnp.min(jnp.where(lg == m1, lanef, float(LANES)), axis=-1, keepdims=True)
    sel1 = lanef == e1
    lg2 = jnp.where(sel1, NEG, lg)
    m2 = jnp.max(lg2, axis=-1, keepdims=True)
    e2 = jnp.min(jnp.where(lg2 == m2, lanef, float(LANES)), axis=-1, keepdims=True)
    sel2 = lanef == e2
    ex = jnp.exp(m2 - m1)
    g1 = 1.0 / (1.0 + ex)
    g2 = ex / (1.0 + ex)

    onehot = jnp.where(sel1 | sel2, 1.0, 0.0)
    rr = lax.broadcasted_iota(jnp.int32, (t, t), 0)
    cc = lax.broadcasted_iota(jnp.int32, (t, t), 1)
    before = jnp.where(cc < rr, 1.0, 0.0).astype(BF16)
    pref = _dot(before, onehot.astype(BF16)) + carry[...]
    r1 = jnp.sum(jnp.where(sel1, pref, 0.0), axis=-1, keepdims=True)
    r2 = jnp.sum(jnp.where(sel2, pref, 0.0), axis=-1, keepdims=True)
    carry[...] = carry[...] + jnp.sum(onehot, axis=0, keepdims=True)

    info = jnp.zeros((t, LANES), F32)
    for col, val in ((_R_E1, e1), (_R_E2, e2), (_R_G1, g1), (_R_G2, g2), (_R_RANK1, r1), (_R_RANK2, r2)):
        info = jnp.where(lane == col, val, info)
    info_ref[...] = info
    cnt_ref[...] = carry[...]


def _route(logits):
    n = logits.shape[0]
    t = ROUTE_TILE
    return pl.pallas_call(
        _route_body,
        grid=(n // t,),
        in_specs=[pl.BlockSpec((t, LANES), lambda i: (i, 0))],
        out_specs=[pl.BlockSpec((t, LANES), lambda i: (i, 0)), pl.BlockSpec((1, LANES), lambda i: (0, 0))],
        out_shape=[jax.ShapeDtypeStruct((n, LANES), F32), jax.ShapeDtypeStruct((1, LANES), F32)],
        scratch_shapes=[pltpu.VMEM((1, LANES), F32)],
        compiler_params=_params(1),
        name="moe_route",
    )(logits)


def _moe_ffn_body(be_ref, nv_ref, stok_ref, sdst_ref, xf_hbm, w1_ref, w3_ref, w2_ref, ys_hbm,
                  xbuf, ybuf, gsem, ssem):
    b = pl.program_id(0)
    last_blk = pl.num_programs(0) - 1
    nv = nv_ref[0]
    cur = b % 2
    nxt = 1 - cur

    def gather_row(blk, buf, r):
        return pltpu.make_async_copy(xf_hbm.at[pl.ds(stok_ref[blk * MOE_BLOCK + r], 1)],
                                     xbuf.at[buf, pl.ds(r, 1)], gsem.at[buf])

    def scatter_row(blk, buf, r):
        return pltpu.make_async_copy(ybuf.at[buf, pl.ds(r, 1)],
                                     ys_hbm.at[pl.ds(sdst_ref[(blk + 1) * MOE_BLOCK + r], 1)], ssem.at[buf])

    def rolled(fn):
        def body(r, carry):
            fn(r)
            return carry
        lax.fori_loop(0, MOE_BLOCK, body, 0, unroll=DMA_UNROLL)

    def inline(fn):
        for r in range(MOE_BLOCK):
            fn(r)

    @pl.when(b == 0)
    def _():
        rolled(lambda r: gather_row(0, 0, r).start())
        ybuf[1] = jnp.zeros((MOE_BLOCK, D_MODEL), F32)

    @pl.when(b < nv)
    def _():
        inline(lambda r: gather_row(b, cur, r).wait())
        x = xbuf[cur].astype(BF16)
        ahead = jnp.minimum(b + 1, last_blk)
        inline(lambda r: gather_row(ahead, nxt, r).start())
        inline(lambda r: scatter_row(b - 1, nxt, r).start())
        acc = jnp.zeros((MOE_BLOCK, D_MODEL), F32)
        ff = w1_ref.shape[1]
        for f0 in range(0, ff, FF_CHUNK_MOE):
            a = _dot(x, w1_ref[:, f0:f0 + FF_CHUNK_MOE])
            g = _dot(x, w3_ref[:, f0:f0 + FF_CHUNK_MOE])
            acc = acc + _dot((a * _sigmoid(a) * g).astype(BF16), w2_ref[f0:f0 + FF_CHUNK_MOE, :])

        @pl.when(b >= 1)
        def _():
            inline(lambda r: scatter_row(b - 2, cur, r).wait())
        ybuf[cur] = acc

        @pl.when(b == nv - 1)
        def _():
            rolled(lambda r: gather_row(ahead, nxt, r).wait())
            rolled(lambda r: scatter_row(b, cur, r).start())
            rolled(lambda r: scatter_row(b - 1, nxt, r).wait())
            rolled(lambda r: scatter_row(b, cur, r).wait())

    @pl.when(b >= nv)
    def _():
        ybuf[0] = jnp.zeros((MOE_BLOCK, D_MODEL), F32)
        rolled(lambda r: scatter_row(b, 0, r).start())
        rolled(lambda r: scatter_row(b, 0, r).wait())


def _moe_ffn(block_e, n_valid, slot_tok, slot_dst, xf, w1, w3, w2, j):
    p = slot_tok.shape[0]
    ff = w1.shape[3]
    spare = p + jnp.arange(MOE_BLOCK, dtype=jnp.int32)
    slot_dst = jnp.concatenate([spare, slot_dst])
    any_spec = pl.BlockSpec(memory_space=pl.ANY)
    wsel = lambda b, be, nv, st, sd: (j, be[b], 0, 0)
    return pl.pallas_call(
        _moe_ffn_body,
        grid_spec=pltpu.PrefetchScalarGridSpec(
            num_scalar_prefetch=4,
            grid=(p // MOE_BLOCK,),
            in_specs=[
                any_spec,
                _resident((None, None, D_MODEL, ff), wsel),
                _resident((None, None, D_MODEL, ff), wsel),
                _resident((None, None, ff, D_MODEL), wsel),
            ],
            out_specs=any_spec,
            scratch_shapes=[
                pltpu.VMEM((2, MOE_BLOCK, D_MODEL), F32),
                pltpu.VMEM((2, MOE_BLOCK, D_MODEL), F32),
                pltpu.SemaphoreType.DMA((2,)),
                pltpu.SemaphoreType.DMA((2,)),
            ],
        ),
        out_shape=jax.ShapeDtypeStruct((p + MOE_BLOCK, D_MODEL), F32),
        compiler_params=pltpu.CompilerParams(dimension_semantics=("arbitrary",), vmem_limit_bytes=VMEM_LIMIT,
                                             has_side_effects=True),
        name="moe_ffn",
    )(block_e, n_valid, slot_tok, slot_dst, xf, w1, w3, w2)


def _combine_body(final_norm, h_ref, info_ref, y1_ref, y2_ref, *rest):
    if final_norm:
        gfin_ref, out_ref = rest
    else:
        (out_ref,) = rest
    info = info_ref[...]
    out = h_ref[...] + (info[:, _R_G1:_R_G1 + 1] * y1_ref[...] + info[:, _R_G2:_R_G2 + 1] * y2_ref[...])
    out_ref[...] = _rms(out, gfin_ref[...]) if final_norm else out


def _combine(h, info, ys, final_g=None):
    n = h.shape[0]
    t = ROW_TILE
    nt = n // t
    row = lambda i: (i, 0)
    in_specs = [
        pl.BlockSpec((t, D_MODEL), row),
        pl.BlockSpec((t, LANES), row),
        pl.BlockSpec((t, D_MODEL), row),
        pl.BlockSpec((t, D_MODEL), lambda i: (i + nt, 0)),
    ]
    args = [h, info, ys, ys]
    if final_g is not None:
        in_specs.append(_resident((1, D_MODEL), lambda i: (0, 0)))
        args.append(final_g)
    return pl.pallas_call(
        functools.partial(_combine_body, final_g is not None),
        grid=(nt,),
        in_specs=in_specs,
        out_specs=pl.BlockSpec((t, D_MODEL), row),
        out_shape=jax.ShapeDtypeStruct((n, D_MODEL), F32),
        compiler_params=_params(1),
        name="moe_combine",
    )(*args)


def _moe(h2, xf, logits, w1, w3, w2, j, final_g=None):
    n = h2.shape[0]
    info, counts = _route(logits)
    cnt = counts[0, :N_EXPERTS].astype(jnp.int32)
    padded = (cnt + MOE_BLOCK - 1) // MOE_BLOCK * MOE_BLOCK
    ends = jnp.cumsum(padded)
    starts = ends - padded
    n_blocks = (n * TOP_K) // MOE_BLOCK + N_EXPERTS
    block_e = jnp.minimum(jnp.searchsorted(ends, jnp.arange(n_blocks, dtype=jnp.int32) * MOE_BLOCK, side="right"),
                          N_EXPERTS - 1).astype(jnp.int32)
    n_valid = (ends[-1:] // MOE_BLOCK).astype(jnp.int32)
    e1 = info[:, _R_E1].astype(jnp.int32)
    e2 = info[:, _R_E2].astype(jnp.int32)
    slot = jnp.concatenate([starts[e1] + info[:, _R_RANK1].astype(jnp.int32),
                            starts[e2] + info[:, _R_RANK2].astype(jnp.int32)])
    n_asg = n * TOP_K
    p = n_blocks * MOE_BLOCK
    asg_of_slot = jnp.full((p,), -1, jnp.int32).at[slot].set(jnp.arange(n_asg, dtype=jnp.int32))
    is_pad = asg_of_slot < 0
    slot_tok = jnp.where(is_pad, 0, asg_of_slot % n)
    slot_dst = jnp.where(is_pad, n_asg - 1 + jnp.cumsum(is_pad.astype(jnp.int32)), asg_of_slot)
    ys = _moe_ffn(block_e, n_valid, slot_tok, slot_dst, xf, w1, w3, w2, j)
    return _combine(h2, info, ys, final_g)


def _row3(a):
    return a.reshape(a.shape[0], 1, a.shape[1])


def kernel(x, mem, mem_norm_g, norm_mix_g, w_in, b_in, conv_dw_w, conv_dw_b, conv_ln_g, conv_ln_b, w_conv_out,
           pool_w, pool_scale, w_pool_out, mlstm_conv_w, mlstm_conv_b, mlstm_norm_g, w_mlstm_out, w_out,
           norm_xattn_g, xa_wq, xa_wk, xa_wv, xa_wo, norm_ffn_g, ffn_w1, ffn_w3, ffn_w2, router_w,
           moe_w1, moe_w3, moe_w2, final_norm_g):
    bsz, s, d = x.shape
    n = bsz * s
    depth = w_in.shape[0]
    bf = lambda a: a.astype(BF16)

    w_main = bf(w_in[:, :, :_M_IF])
    b_main = _row3(b_in[:, :_M_IF])
    pad_if = LANES - 2 * HEADS
    w_if = bf(jnp.pad(w_in[:, :, _M_IF:_G_PRE], ((0, 0), (0, 0), (0, pad_if))))
    b_if = _row3(jnp.pad(b_in[:, _M_IF:_G_PRE], ((0, 0), (0, pad_if))))
    w_g01 = bf(w_in[:, :, _G_PRE:_G_PRE + 2 * D_MODEL])
    b_g01 = _row3(b_in[:, _G_PRE:_G_PRE + 2 * D_MODEL])
    w_g2 = bf(w_in[:, :, _G_PRE + 2 * D_MODEL:])
    b_g2 = _row3(b_in[:, _G_PRE + 2 * D_MODEL:])
    gmix = _row3(norm_mix_g)
    gxa = _row3(norm_xattn_g)
    gffn = _row3(norm_ffn_g)
    final_g = final_norm_g.reshape(1, d)
    conv_b3, ln_g3, ln_b3 = _row3(conv_dw_b), _row3(conv_ln_g), _row3(conv_ln_b)
    w_co, w_po, pool_wb = bf(w_conv_out), bf(w_pool_out), bf(pool_w)
    pool_s3 = _row3(pool_scale)
    mconv_b3, mnorm_g3 = _row3(mlstm_conv_b), _row3(mlstm_norm_g)
    w_mo, w_o = bf(w_mlstm_out), bf(w_out)
    wq, wk, wv, wo = bf(xa_wq), bf(xa_wk), bf(xa_wv), bf(xa_wo)
    f1, f3, f2 = bf(ffn_w1), bf(ffn_w3), bf(ffn_w2)
    m1, m3, m2 = bf(moe_w1), bf(moe_w3), bf(moe_w2)
    w_router = jnp.pad(router_w, ((0, 0), (0, 0), (0, LANES - N_EXPERTS)))
    w_router_hi = bf(w_router)
    w_router_lo = bf(w_router - w_router_hi.astype(F32))

    kmem, vmem = _memkv(mem, mem_norm_g.reshape(1, d), wk, wv)

    h = x
    for l in range(depth):
        last = l == depth - 1
        cv, pu, q, k, v, z, gif = _inproj(h, gmix, w_main, b_main, w_if, b_if, conv_dw_w, conv_b3,
                                          mlstm_conv_w, mconv_b3, l)
        yab = _branches(h, gmix, cv, pu, ln_g3, ln_b3, w_co, pool_wb, pool_s3, w_po, w_g01, b_g01, l)
        hz = _mlstm(q, k, v, z, gif, mnorm_g3, l)
        j = l // 2
        if l % 2 == 0:
            (h2,) = _merge_xattn(h, gmix, yab, hz, w_g2, b_g2, w_mo, w_o, gxa, wq, kmem, vmem, wo, l)
            hnext = _ffn(h2.reshape(n, d), gffn, f1, f3, f2, l, j, final_g if last else None)
        else:
            h2, xf, logits = _merge_xattn(h, gmix, yab, hz, w_g2, b_g2, w_mo, w_o, gxa, wq, kmem, vmem, wo, l,
                                          router=(gffn, w_router_hi, w_router_lo, j))
            hnext = _moe(h2.reshape(n, d), xf.reshape(n, d), logits.reshape(n, LANES), m1, m3, m2, j,
                         final_g if last else None)
        h = hnext.reshape(bsz, s, d)
    return h
```

```python
import functools

import jax
import jax.numpy as jnp
from jax import lax
from jax.experimental import pallas as pl
from jax.experimental.pallas import tpu as pltpu

F32 = jnp.float32
BF16 = jnp.bfloat16
EPS = 1e-6
NEG = -1e30

D_MODEL = 1024
CONV_W = 512
CONV_K = 31
POOL_W = 512
POOL_GW = 128
POOL_WINDOWS = (2, 4, 8, 16)
MLSTM_W = 1024
HEADS = 4
DH = 256
MLSTM_CONV_K = 4
XA_HEADS = 4
XA_DH = 256
N_EXPERTS = 8
TOP_K = 2

LANES = 128
SUBLANES = 8
VMEM_LIMIT = 56 * 1024 * 1024

ROW_TILE = 512
IN_HALO = 32
POOL_HALO = 16
QK_HALO = 16
VPU_ROWS = 64
MLSTM_CHUNK = 512
MOE_BLOCK = 512
FF_CHUNK_DENSE = 1408
FF_CHUNK_MOE = 896
ROUTE_TILE = 512
DMA_UNROLL = 8

_C_A = 0
_C_B = _C_A + CONV_W
_P_U = _C_B + CONV_W
_M_QK = _P_U + POOL_W
_M_V = _M_QK + 2 * MLSTM_W
_M_Z = _M_V + MLSTM_W
_M_IF = _M_Z + MLSTM_W
_G_PRE = _M_IF + 2 * HEADS
_W_IF = _M_IF
_W_G01 = -(-(_W_IF + LANES) // (2 * D_MODEL)) * (2 * D_MODEL)
_W_G2 = _W_G01 + 2 * D_MODEL


def _params(n_axes=1):
    return pltpu.CompilerParams(dimension_semantics=("arbitrary",) * n_axes, vmem_limit_bytes=VMEM_LIMIT)


def _resident(block_shape, index_map):
    return pl.BlockSpec(block_shape, index_map, pipeline_mode=pl.Buffered(1))


def _rms(x, g):
    return x * lax.rsqrt(jnp.mean(x * x, axis=-1, keepdims=True) + EPS) * g


def _sigmoid(x):
    return 1.0 / (1.0 + jnp.exp(-x))


def _dot(a, b):
    return jnp.dot(a, b, preferred_element_type=F32)


def _inproj_body(h_ref, hprev_ref, g_ref, w_ref, b_ref, wif_ref, bif_ref, cw_ref, cb_ref, mw_ref, mb_ref,
                 cv_ref, pu_ref, q_ref, k_ref, v_ref, z_ref, if_ref,
                 xn_ext, u_ext, shifted, qk_ext):
    t = h_ref.shape[0]
    has_prev = pl.program_id(1) > 0

    xn_ext[0:IN_HALO, :] = _rms(hprev_ref[...], g_ref[...]).astype(BF16)
    xn_ext[IN_HALO:IN_HALO + t, :] = _rms(h_ref[...], g_ref[...]).astype(BF16)

    def proj(x, lo, hi):
        return _dot(x, w_ref[:, lo:hi]) + b_ref[:, lo:hi]

    def causal_input(y, halo):
        row = lax.broadcasted_iota(jnp.int32, y.shape, 0)
        return jnp.where(jnp.logical_or(has_prev, row >= halo), y, 0.0)

    def conv31_lanes(c0):
        base = IN_HALO - (CONV_K - 1)
        for phase in range(SUBLANES):
            rows = t + (CONV_K - 1 - phase) // SUBLANES * SUBLANES
            shifted[c0 // LANES % 2, phase, 0:rows, :] = u_ext[base + phase:base + phase + rows, c0:c0 + LANES]
        for r0 in range(0, t, VPU_ROWS):
            acc = jnp.broadcast_to(cb_ref[:, c0:c0 + LANES], (VPU_ROWS, LANES))
            for j in range(CONV_K):
                phase, off = j % SUBLANES, j // SUBLANES * SUBLANES
                acc = acc + cw_ref[j:j + 1, c0:c0 + LANES] * shifted[c0 // LANES % 2, phase, r0 + off:r0 + off + VPU_ROWS, :]
            cv_ref[r0:r0 + VPU_ROWS, c0:c0 + LANES] = acc

    def qk_project(half):
        lo = _M_QK + half * MLSTM_W
        xq = xn_ext[IN_HALO - QK_HALO:IN_HALO + t, :]
        qk_ext[half] = causal_input(proj(xq, lo, lo + MLSTM_W), QK_HALO)

    def qk_conv(half, out_ref):
        base = QK_HALO - (MLSTM_CONV_K - 1)
        cols = slice(half * MLSTM_W, (half + 1) * MLSTM_W)
        conv = jnp.broadcast_to(mb_ref[:, cols], (t, MLSTM_W))
        for j in range(MLSTM_CONV_K):
            conv = conv + mw_ref[j:j + 1, cols] * qk_ext[half, base + j:base + j + t, :]
        out_ref[...] = (conv * _sigmoid(conv)).astype(out_ref.dtype)

    xe = xn_ext[...]
    xn = xn_ext[IN_HALO:IN_HALO + t, :]
    u_ext[...] = causal_input(proj(xe, _C_A, _C_B) * _sigmoid(proj(xe, _C_B, _P_U)), IN_HALO)
    qk_project(0)
    conv31_lanes(0 * LANES)
    qk_project(1)
    qk_conv(0, q_ref)
    pu_ref[...] = proj(xn, _P_U, _M_QK)
    conv31_lanes(1 * LANES)
    v_ref[...] = proj(xn, _M_V, _M_Z).astype(BF16)
    qk_conv(1, k_ref)
    z_ref[...] = proj(xn, _M_Z, _M_IF)
    conv31_lanes(2 * LANES)
    if_ref[...] = _dot(xn, wif_ref[...]) + bif_ref[...]
    conv31_lanes(3 * LANES)


def _inproj(h, gmix, w_all, b_all, conv_w, conv_b, mconv_w, mconv_b, l):
    bsz, s, _ = h.shape
    t = ROW_TILE
    lsel = lambda b, i: (l, 0, 0)
    cur = lambda b, i: (b, i, 0)
    prev = lambda b, i: (b, jnp.maximum(i * (t // IN_HALO) - 1, 0), 0)
    widths = (CONV_W, POOL_W, MLSTM_W, MLSTM_W, MLSTM_W, MLSTM_W, LANES)
    dtypes = (F32, F32, BF16, F32, BF16, F32, F32)
    return pl.pallas_call(
        _inproj_body,
        grid=(bsz, s // t),
        in_specs=[
            pl.BlockSpec((None, t, D_MODEL), cur),
            pl.BlockSpec((None, IN_HALO, D_MODEL), prev),
            _resident((None, 1, D_MODEL), lsel),
            _resident((None, D_MODEL, _M_IF), lsel),
            _resident((None, 1, _M_IF), lsel),
            _resident((None, D_MODEL, LANES), lambda b, i: (l, 0, _W_IF // LANES)),
            _resident((None, 1, LANES), lambda b, i: (l, 0, _W_IF // LANES)),
            _resident((None, CONV_K, CONV_W), lsel),
            _resident((None, 1, CONV_W), lsel),
            _resident((None, MLSTM_CONV_K, 2 * MLSTM_W), lsel),
            _resident((None, 1, 2 * MLSTM_W), lsel),
        ],
        out_specs=[pl.BlockSpec((None, t, w), cur) for w in widths],
        out_shape=[jax.ShapeDtypeStruct((bsz, s, w), dt) for w, dt in zip(widths, dtypes)],
        scratch_shapes=[
            pltpu.VMEM((IN_HALO + t, D_MODEL), BF16),
            pltpu.VMEM((IN_HALO + t, CONV_W), F32),
            pltpu.VMEM((2, SUBLANES, t + (CONV_K - 1) // SUBLANES * SUBLANES, LANES), F32),
            pltpu.VMEM((2, QK_HALO + t, MLSTM_W), F32),
        ],
        compiler_params=_params(2),
        name="inproj",
    )(h, h, gmix, w_all, b_all, w_all, b_all, conv_w, conv_b, mconv_w, mconv_b)


def _branches_body(h_ref, g_ref, cv_ref, pu_ref, puprev_ref,
                   lng_ref, lnb_ref, wco_ref,
                   pw_ref, ps_ref, wpo_ref, wg_ref, bg_ref,
                   yab_ref, extp, pools):
    t = pu_ref.shape[0]
    i = pl.program_id(1)
    has_prev = i > 0

    cv = cv_ref[...]
    mu = jnp.mean(cv, axis=-1, keepdims=True)
    var = jnp.mean(jnp.square(cv - mu), axis=-1, keepdims=True)
    a = (cv - mu) * lax.rsqrt(var + EPS) * lng_ref[...] + lnb_ref[...]
    a = a * _sigmoid(a)
    ya = _dot(a.astype(BF16), wco_ref[...])

    extp[0:POOL_HALO, :] = jnp.where(has_prev, puprev_ref[...], 0.0)
    extp[POOL_HALO:POOL_HALO + t, :] = pu_ref[...]
    for g, window in enumerate(POOL_WINDOWS):
        c0 = g * POOL_GW
        for r0 in range(0, t, VPU_ROWS):
            own = extp[POOL_HALO + r0:POOL_HALO + r0 + VPU_ROWS, c0:c0 + POOL_GW]
            acc = own
            for k in range(1, window):
                acc = acc + extp[POOL_HALO + r0 - k:POOL_HALO + r0 - k + VPU_ROWS, c0:c0 + POOL_GW]
            pos = i * t + r0 + lax.broadcasted_iota(jnp.int32, (VPU_ROWS, POOL_GW), 0)
            cnt = jnp.minimum(pos + 1, window).astype(F32)
            pools[r0:r0 + VPU_ROWS, c0:c0 + POOL_GW] = acc / cnt - own
    yp = jnp.concatenate(
        [_dot(pools[:, g * POOL_GW:(g + 1) * POOL_GW].astype(BF16), pw_ref[g]) for g in range(len(POOL_WINDOWS))],
        axis=1) * ps_ref[...]
    yb = _dot(yp.astype(BF16), wpo_ref[...])

    xn = _rms(h_ref[...], g_ref[...]).astype(BF16)
    gates = _sigmoid(_dot(xn, wg_ref[...]) + bg_ref[...])
    yab_ref[...] = gates[:, :D_MODEL] * ya + gates[:, D_MODEL:] * yb


def _branches(h, gmix, cv, pu, ln_g, ln_b, w_conv_out, pool_w, pool_scale, w_pool_out, w_gate, b_gate, l):
    bsz, s, _ = h.shape
    t = ROW_TILE
    lsel3 = lambda b, i: (l, 0, 0)
    lsel4 = lambda b, i: (l, 0, 0, 0)
    cur = lambda b, i: (b, i, 0)
    prev_p = lambda b, i: (b, jnp.maximum(i * (t // POOL_HALO) - 1, 0), 0)
    return pl.pallas_call(
        _branches_body,
        grid=(bsz, s // t),
        in_specs=[
            pl.BlockSpec((None, t, D_MODEL), cur),
            _resident((None, 1, D_MODEL), lsel3),
            pl.BlockSpec((None, t, CONV_W), cur),
            pl.BlockSpec((None, t, POOL_W), cur),
            pl.BlockSpec((None, POOL_HALO, POOL_W), prev_p),
            _resident((None, 1, CONV_W), lsel3),
            _resident((None, 1, CONV_W), lsel3),
            _resident((None, CONV_W, D_MODEL), lsel3),
            _resident((None, len(POOL_WINDOWS), POOL_GW, POOL_GW), lsel4),
            _resident((None, 1, POOL_W), lsel3),
            _resident((None, POOL_W, D_MODEL), lsel3),
            _resident((None, D_MODEL, 2 * D_MODEL), lambda b, i: (l, 0, _W_G01 // (2 * D_MODEL))),
            _resident((None, 1, 2 * D_MODEL), lambda b, i: (l, 0, _W_G01 // (2 * D_MODEL))),
        ],
        out_specs=pl.BlockSpec((None, t, D_MODEL), cur),
        out_shape=jax.ShapeDtypeStruct((bsz, s, D_MODEL), F32),
        scratch_shapes=[
            pltpu.VMEM((POOL_HALO + t, POOL_W), F32),
            pltpu.VMEM((t, POOL_W), F32),
        ],
        compiler_params=_params(2),
        name="branches",
    )(h, gmix, cv, pu, pu, ln_g, ln_b, w_conv_out, pool_w, pool_scale, w_pool_out, w_gate, b_gate)


def _cumsum_rows(x):
    n = x.shape[0]
    row = lax.broadcasted_iota(jnp.int32, x.shape, 0)
    shift = 1
    while shift < n:
        x = x + jnp.where(row >= shift, pltpu.roll(x, shift, axis=0), 0.0)
        shift *= 2
    return x


def _mlstm_body(q_ref, k_ref, v_ref, z_ref, if_ref, ng_ref, hz_ref, ct_scr, n_scr, m_scr):
    ln = q_ref.shape[0]
    c = pl.program_id(1)

    @pl.when(c == 0)
    def _():
        ct_scr[...] = jnp.zeros_like(ct_scr)
        n_scr[...] = jnp.zeros_like(n_scr)
        m_scr[...] = jnp.zeros_like(m_scr)

    gates = if_ref[...]
    logf = jnp.minimum(gates, 0.0) - jnp.log(1.0 + jnp.exp(-jnp.abs(gates)))
    bcum = _cumsum_rows(logf)
    lane = lax.broadcasted_iota(jnp.int32, gates.shape, 1)
    rows_t = jnp.where(lane < HEADS, gates, bcum).T

    tt = lax.broadcasted_iota(jnp.int32, (ln, ln), 0)
    ss = lax.broadcasted_iota(jnp.int32, (ln, ln), 1)
    causal = ss <= tt

    for hd in range(HEADS):
        qb = q_ref[:, hd * DH:(hd + 1) * DH]
        q = qb.astype(F32)
        k = k_ref[:, hd * DH:(hd + 1) * DH] * (DH ** -0.5)
        v = v_ref[:, hd * DH:(hd + 1) * DH]
        i_col = gates[:, hd:hd + 1]
        b_col = bcum[:, HEADS + hd:HEADS + hd + 1]
        i_row = rows_t[hd:hd + 1, :]
        b_row = rows_t[HEADS + hd:HEADS + hd + 1, :]
        m_prev = m_scr[hd][:, 0:1]
        ct = ct_scr[hd]
        nvec = n_scr[hd]

        log_d = jnp.where(causal, b_col - b_row + i_row, NEG)
        inter = b_col + m_prev
        mt = jnp.maximum(inter, jnp.max(log_d, axis=-1, keepdims=True))
        dmat = jnp.exp(log_d - mt)
        a_inter = jnp.exp(inter - mt)
        kb = k.astype(BF16)
        sc = lax.dot_general(qb, kb, (((1,), (1,)), ((), ())), preferred_element_type=F32) * dmat
        num = _dot(sc.astype(BF16), v) + a_inter * _dot(qb, ct.astype(BF16))
        den = jnp.sum(sc, axis=-1, keepdims=True) + a_inter * jnp.sum(q * nvec, axis=-1, keepdims=True)
        hh = num / jnp.maximum(jnp.abs(den), jnp.exp(-mt))

        b_last = b_col[ln - 1:ln, :]
        w_log = b_last - b_col + i_col
        m_new = jnp.maximum(b_last + m_prev, jnp.max(w_log, axis=0, keepdims=True))
        a_c = jnp.exp(b_last + m_prev - m_new)
        kw = k * jnp.exp(w_log - m_new)
        ct_scr[hd] = a_c * ct + lax.dot_general(kw.astype(BF16), v, (((0,), (0,)), ((), ())),
                                                preferred_element_type=F32)
        n_scr[hd] = a_c * nvec + jnp.sum(kw, axis=0, keepdims=True)
        m_scr[hd] = jnp.broadcast_to(m_new, (1, LANES))

        hn = hh * lax.rsqrt(jnp.mean(hh * hh, axis=-1, keepdims=True) + EPS) * ng_ref[:, hd * DH:(hd + 1) * DH]
        hz_ref[:, hd * DH:(hd + 1) * DH] = (hn * _sigmoid(z_ref[:, hd * DH:(hd + 1) * DH])).astype(BF16)


def _mlstm(q, k, v, z, gates_if, norm_g, l):
    bsz, s, _ = q.shape
    ln = MLSTM_CHUNK
    lsel3 = lambda b, c: (l, 0, 0)
    cur = lambda b, c: (b, c, 0)
    wide = pl.BlockSpec((None, ln, MLSTM_W), cur)
    return pl.pallas_call(
        _mlstm_body,
        grid=(bsz, s // ln),
        in_specs=[wide, wide, wide, wide, pl.BlockSpec((None, ln, LANES), cur), _resident((None, 1, MLSTM_W), lsel3)],
        out_specs=wide,
        out_shape=jax.ShapeDtypeStruct((bsz, s, MLSTM_W), BF16),
        scratch_shapes=[
            pltpu.VMEM((HEADS, DH, DH), F32),
            pltpu.VMEM((HEADS, 1, DH), F32),
            pltpu.VMEM((HEADS, 1, LANES), F32),
        ],
        compiler_params=_params(2),
        name="mlstm",
    )(q, k, v, z, gates_if, norm_g)


def _merge_xattn_body(with_router, h_ref, gmix_ref, yab_ref, hz_ref, wg_ref, bg_ref, wmo_ref, wout_ref,
                      gxa_ref, wq_ref, k_ref, v_ref, wo_ref, *rest):
    if with_router:
        gffn_ref, wr_ref, h2_ref, xf_ref, lg_ref = rest
    else:
        (h2_ref,) = rest
    h = h_ref[...]
    xn = _rms(h, gmix_ref[...]).astype(BF16)
    g2 = _sigmoid(_dot(xn, wg_ref[...]) + bg_ref[...])
    y = yab_ref[...] + g2 * _dot(hz_ref[...], wmo_ref[...])
    h1 = h + _dot(y.astype(BF16), wout_ref[...])

    hn = _rms(h1, gxa_ref[...]).astype(BF16)
    q = _dot(hn, wq_ref[...])
    heads = []
    for hd in range(XA_HEADS):
        sl = slice(hd * XA_DH, (hd + 1) * XA_DH)
        sc = lax.dot_general(q[:, sl].astype(BF16), k_ref[:, sl], (((1,), (1,)), ((), ())),
                             preferred_element_type=F32) * (XA_DH ** -0.5)
        e = jnp.exp(sc - jnp.max(sc, axis=-1, keepdims=True))
        p = e / jnp.sum(e, axis=-1, keepdims=True)
        heads.append(_dot(p.astype(BF16), v_ref[:, sl]))
    o = jnp.concatenate(heads, axis=1)
    h2 = h1 + _dot(o.astype(BF16), wo_ref[...])
    h2_ref[...] = h2
    if with_router:
        xf = _rms(h2, gffn_ref[...])
        xf_ref[...] = xf
        t = xf.shape[0]
        xf_hi = xf.astype(BF16)
        xf_lo = (xf - xf_hi.astype(F32)).astype(BF16)
        r = _dot(jnp.concatenate([xf_hi, xf_lo], axis=0), wr_ref[...])
        lg_ref[...] = r[:t, :LANES] + (r[t:, :LANES] + r[:t, LANES:])


def _merge_xattn(h, gmix, yab, hz, w_all, b_all, w_mo, w_out, gxa, wq, kmem, vmem, wo, l,
                 router=None):
    bsz, s, _ = h.shape
    t = ROW_TILE
    m = kmem.shape[2]
    lsel3 = lambda b, i: (l, 0, 0)
    cur = lambda b, i: (b, i, 0)
    mem = lambda b, i: (l, b, 0, 0)
    gsel = lambda b, i: (l, 0, _W_G2 // D_MODEL)
    sq = (None, D_MODEL, D_MODEL)
    in_specs = [
        pl.BlockSpec((None, t, D_MODEL), cur),
        _resident((None, 1, D_MODEL), lsel3),
        pl.BlockSpec((None, t, D_MODEL), cur),
        pl.BlockSpec((None, t, D_MODEL), cur),
        _resident(sq, gsel),
        _resident((None, 1, D_MODEL), gsel),
        _resident(sq, lsel3),
        _resident(sq, lsel3),
        _resident((None, 1, D_MODEL), lsel3),
        _resident(sq, lsel3),
        pl.BlockSpec((None, None, m, D_MODEL), mem),
        pl.BlockSpec((None, None, m, D_MODEL), mem),
        _resident(sq, lsel3),
    ]
    args = [h, gmix, yab, hz, w_all, b_all, w_mo, w_out, gxa, wq, kmem, vmem, wo]
    out_specs = [pl.BlockSpec((None, t, D_MODEL), cur)]
    out_shape = [jax.ShapeDtypeStruct((bsz, s, D_MODEL), F32)]
    if router is not None:
        gffn, w_router, j = router
        in_specs += [_resident((None, 1, D_MODEL), lsel3),
                     _resident((None, D_MODEL, 2 * LANES), lambda b, i: (j, 0, 0))]
        args += [gffn, w_router]
        out_specs += [pl.BlockSpec((None, t, D_MODEL), cur), pl.BlockSpec((None, t, LANES), cur)]
        out_shape += [jax.ShapeDtypeStruct((bsz, s, D_MODEL), F32), jax.ShapeDtypeStruct((bsz, s, LANES), F32)]
    return pl.pallas_call(
        functools.partial(_merge_xattn_body, router is not None),
        grid=(bsz, s // t),
        in_specs=in_specs,
        out_specs=out_specs,
        out_shape=out_shape,
        compiler_params=_params(2),
        name="merge_xattn_router" if router is not None else "merge_xattn",
    )(*args)


def _memkv_body(mem_ref, g_ref, wk_ref, wv_ref, k_ref, v_ref):
    mn = _rms(mem_ref[...], g_ref[...]).astype(BF16)
    k_ref[...] = _dot(mn, wk_ref[...]).astype(BF16)
    v_ref[...] = _dot(mn, wv_ref[...]).astype(BF16)


def _memkv(mem, g, wk, wv):
    bsz, m, _ = mem.shape
    depth = wk.shape[0]
    out = jax.ShapeDtypeStruct((depth, bsz, m, D_MODEL), BF16)
    return pl.pallas_call(
        _memkv_body,
        grid=(depth, bsz),
        in_specs=[
            pl.BlockSpec((None, m, D_MODEL), lambda l, b: (b, 0, 0)),
            pl.BlockSpec((1, D_MODEL), lambda l, b: (0, 0)),
            pl.BlockSpec((None, D_MODEL, D_MODEL), lambda l, b: (l, 0, 0)),
            pl.BlockSpec((None, D_MODEL, D_MODEL), lambda l, b: (l, 0, 0)),
        ],
        out_specs=[pl.BlockSpec((None, None, m, D_MODEL), lambda l, b: (l, b, 0, 0))] * 2,
        out_shape=[out, out],
        compiler_params=_params(2),
        name="memkv",
    )(mem, g, wk, wv)


def _ffn_body(final_norm, h_ref, g_ref, w1_ref, w3_ref, w2_ref, *rest):
    if final_norm:
        gfin_ref, out_ref = rest
    else:
        (out_ref,) = rest
    h = h_ref[...]
    xn = _rms(h, g_ref[...]).astype(BF16)
    acc = h
    ff = w1_ref.shape[1]
    for f0 in range(0, ff, FF_CHUNK_DENSE):
        a = _dot(xn, w1_ref[:, f0:f0 + FF_CHUNK_DENSE])
        b = _dot(xn, w3_ref[:, f0:f0 + FF_CHUNK_DENSE])
        acc = acc + _dot((a * _sigmoid(a) * b).astype(BF16), w2_ref[f0:f0 + FF_CHUNK_DENSE, :])
    out_ref[...] = _rms(acc, gfin_ref[...]) if final_norm else acc


def _ffn(h, gffn, w1, w3, w2, l, j, final_g=None):
    n = h.shape[0]
    t = ROW_TILE
    ff = w1.shape[2]
    row = lambda i: (i, 0)
    in_specs = [
        pl.BlockSpec((t, D_MODEL), row),
        _resident((None, 1, D_MODEL), lambda i: (l, 0, 0)),
        _resident((None, D_MODEL, ff), lambda i: (j, 0, 0)),
        _resident((None, D_MODEL, ff), lambda i: (j, 0, 0)),
        _resident((None, ff, D_MODEL), lambda i: (j, 0, 0)),
    ]
    args = [h, gffn, w1, w3, w2]
    if final_g is not None:
        in_specs.append(_resident((1, D_MODEL), lambda i: (0, 0)))
        args.append(final_g)
    return pl.pallas_call(
        functools.partial(_ffn_body, final_g is not None),
        grid=(n // t,),
        in_specs=in_specs,
        out_specs=pl.BlockSpec((t, D_MODEL), row),
        out_shape=jax.ShapeDtypeStruct((n, D_MODEL), F32),
        compiler_params=_params(1),
        name="ffn_dense",
    )(*args)


_R_E1, _R_E2, _R_G1, _R_G2, _R_RANK1, _R_RANK2 = range(6)


def _route_body(lg_ref, info_ref, cnt_ref, carry):
    t = lg_ref.shape[0]
    i = pl.program_id(0)

    @pl.when(i == 0)
    def _():
        carry[...] = jnp.zeros_like(carry)

    lane = lax.broadcasted_iota(jnp.int32, (t, LANES), 1)
    lanef = lane.astype(F32)
    lg = jnp.where(lane < N_EXPERTS, lg_ref[...], NEG)
    m1 = jnp.max(lg, axis=-1, keepdims=True)
    e1 = jnp.min(jnp.where(lg == m1, lanef, float(LANES)), axis=-1, keepdims=True)
    sel1 = lanef == e1
    lg2 = jnp.where(sel1, NEG, lg)
    m2 = jnp.max(lg2, axis=-1, keepdims=True)
    e2 = jnp.min(jnp.where(lg2 == m2, lanef, float(LANES)), axis=-1, keepdims=True)
    sel2 = lanef == e2
    ex = jnp.exp(m2 - m1)
    g1 = 1.0 / (1.0 + ex)
    g2 = ex / (1.0 + ex)

    onehot = jnp.where(sel1 | sel2, 1.0, 0.0)
    rr = lax.broadcasted_iota(jnp.int32, (t, t), 0)
    cc = lax.broadcasted_iota(jnp.int32, (t, t), 1)
    before = jnp.where(cc < rr, 1.0, 0.0).astype(BF16)
    pref = _dot(before, onehot.astype(BF16)) + carry[...]
    r1 = jnp.sum(jnp.where(sel1, pref, 0.0), axis=-1, keepdims=True)
    r2 = jnp.sum(jnp.where(sel2, pref, 0.0), axis=-1, keepdims=True)
    carry[...] = carry[...] + jnp.sum(onehot, axis=0, keepdims=True)

    info = jnp.zeros((t, LANES), F32)
    for col, val in ((_R_E1, e1), (_R_E2, e2), (_R_G1, g1), (_R_G2, g2), (_R_RANK1, r1), (_R_RANK2, r2)):
        info = jnp.where(lane == col, val, info)
    info_ref[...] = info
    cnt_ref[...] = carry[...]


def _route(logits):
    n = logits.shape[0]
    t = ROUTE_TILE
    return pl.pallas_call(
        _route_body,
        grid=(n // t,),
        in_specs=[pl.BlockSpec((t, LANES), lambda i: (i, 0))],
        out_specs=[pl.BlockSpec((t, LANES), lambda i: (i, 0)), pl.BlockSpec((1, LANES), lambda i: (0, 0))],
        out_shape=[jax.ShapeDtypeStruct((n, LANES), F32), jax.ShapeDtypeStruct((1, LANES), F32)],
        scratch_shapes=[pltpu.VMEM((1, LANES), F32)],
        compiler_params=_params(1),
        name="moe_route",
    )(logits)


def _moe_ffn_body(be_ref, nv_ref, stok_ref, sdst_ref, xf_hbm, w1_ref, w3_ref, w2_ref, ys_hbm,
                  xbuf, ybuf, gsem, ssem):
    b = pl.program_id(0)
    last_blk = pl.num_programs(0) - 1
    nv = nv_ref[0]
    cur = b % 2
    nxt = 1 - cur

    def gather_row(blk, buf, r):
        return pltpu.make_async_copy(xf_hbm.at[pl.ds(stok_ref[blk * MOE_BLOCK + r], 1)],
                                     xbuf.at[buf, pl.ds(r, 1)], gsem.at[buf])

    def scatter_row(blk, buf, r):
        return pltpu.make_async_copy(ybuf.at[buf, pl.ds(r, 1)],
                                     ys_hbm.at[pl.ds(sdst_ref[(blk + 1) * MOE_BLOCK + r], 1)], ssem.at[buf])

    def rolled(fn):
        def body(r, carry):
            fn(r)
            return carry
        lax.fori_loop(0, MOE_BLOCK, body, 0, unroll=DMA_UNROLL)

    def inline(fn):
        for r in range(MOE_BLOCK):
            fn(r)

    @pl.when(b == 0)
    def _():
        rolled(lambda r: gather_row(0, 0, r).start())
        ybuf[1] = jnp.zeros((MOE_BLOCK, D_MODEL), F32)

    @pl.when(b < nv)
    def _():
        inline(lambda r: gather_row(b, cur, r).wait())
        x = xbuf[cur].astype(BF16)
        ahead = jnp.minimum(b + 1, last_blk)
        inline(lambda r: gather_row(ahead, nxt, r).start())
        inline(lambda r: scatter_row(b - 1, nxt, r).start())
        acc = jnp.zeros((MOE_BLOCK, D_MODEL), F32)
        ff = w1_ref.shape[1]
        for f0 in range(0, ff, FF_CHUNK_MOE):
            a = _dot(x, w1_ref[:, f0:f0 + FF_CHUNK_MOE])
            g = _dot(x, w3_ref[:, f0:f0 + FF_CHUNK_MOE])
            acc = acc + _dot((a * _sigmoid(a) * g).astype(BF16), w2_ref[f0:f0 + FF_CHUNK_MOE, :])

        @pl.when(b >= 1)
        def _():
            inline(lambda r: scatter_row(b - 2, cur, r).wait())
        ybuf[cur] = acc

        @pl.when(b == nv - 1)
        def _():
            rolled(lambda r: gather_row(ahead, nxt, r).wait())
            rolled(lambda r: scatter_row(b, cur, r).start())
            rolled(lambda r: scatter_row(b - 1, nxt, r).wait())
            rolled(lambda r: scatter_row(b, cur, r).wait())

    @pl.when(b >= nv)
    def _():
        ybuf[0] = jnp.zeros((MOE_BLOCK, D_MODEL), F32)
        rolled(lambda r: scatter_row(b, 0, r).start())
        rolled(lambda r: scatter_row(b, 0, r).wait())


def _moe_ffn(block_e, n_valid, slot_tok, slot_dst, xf, w1, w3, w2, j):
    p = slot_tok.shape[0]
    ff = w1.shape[3]
    spare = p + jnp.arange(MOE_BLOCK, dtype=jnp.int32)
    slot_dst = jnp.concatenate([spare, slot_dst])
    any_spec = pl.BlockSpec(memory_space=pl.ANY)
    wsel = lambda b, be, nv, st, sd: (j, be[b], 0, 0)
    return pl.pallas_call(
        _moe_ffn_body,
        grid_spec=pltpu.PrefetchScalarGridSpec(
            num_scalar_prefetch=4,
            grid=(p // MOE_BLOCK,),
            in_specs=[
                any_spec,
                _resident((None, None, D_MODEL, ff), wsel),
                _resident((None, None, D_MODEL, ff), wsel),
                _resident((None, None, ff, D_MODEL), wsel),
            ],
            out_specs=any_spec,
            scratch_shapes=[
                pltpu.VMEM((2, MOE_BLOCK, D_MODEL), F32),
                pltpu.VMEM((2, MOE_BLOCK, D_MODEL), F32),
                pltpu.SemaphoreType.DMA((2,)),
                pltpu.SemaphoreType.DMA((2,)),
            ],
        ),
        out_shape=jax.ShapeDtypeStruct((p + MOE_BLOCK, D_MODEL), F32),
        compiler_params=pltpu.CompilerParams(dimension_semantics=("arbitrary",), vmem_limit_bytes=VMEM_LIMIT,
                                             has_side_effects=True),
        name="moe_ffn",
    )(block_e, n_valid, slot_tok, slot_dst, xf, w1, w3, w2)


def _combine_body(final_norm, h_ref, info_ref, y1_ref, y2_ref, *rest):
    if final_norm:
        gfin_ref, out_ref = rest
    else:
        (out_ref,) = rest
    info = info_ref[...]
    out = h_ref[...] + (info[:, _R_G1:_R_G1 + 1] * y1_ref[...] + info[:, _R_G2:_R_G2 + 1] * y2_ref[...])
    out_ref[...] = _rms(out, gfin_ref[...]) if final_norm else out


def _combine(h, info, ys, final_g=None):
    n = h.shape[0]
    t = ROW_TILE
    nt = n // t
    row = lambda i: (i, 0)
    in_specs = [
        pl.BlockSpec((t, D_MODEL), row),
        pl.BlockSpec((t, LANES), row),
        pl.BlockSpec((t, D_MODEL), row),
        pl.BlockSpec((t, D_MODEL), lambda i: (i + nt, 0)),
    ]
    args = [h, info, ys, ys]
    if final_g is not None:
        in_specs.append(_resident((1, D_MODEL), lambda i: (0, 0)))
        args.append(final_g)
    return pl.pallas_call(
        functools.partial(_combine_body, final_g is not None),
        grid=(nt,),
        in_specs=in_specs,
        out_specs=pl.BlockSpec((t, D_MODEL), row),
        out_shape=jax.ShapeDtypeStruct((n, D_MODEL), F32),
        compiler_params=_params(1),
        name="moe_combine",
    )(*args)


def _moe(h2, xf, logits, w1, w3, w2, j, final_g=None):
    n = h2.shape[0]
    info, counts = _route(logits)
    cnt = counts[0, :N_EXPERTS].astype(jnp.int32)
    padded = (cnt + MOE_BLOCK - 1) // MOE_BLOCK * MOE_BLOCK
    ends = jnp.cumsum(padded)
    starts = ends - padded
    n_blocks = (n * TOP_K) // MOE_BLOCK + N_EXPERTS
    block_e = jnp.minimum(jnp.searchsorted(ends, jnp.arange(n_blocks, dtype=jnp.int32) * MOE_BLOCK, side="right"),
                          N_EXPERTS - 1).astype(jnp.int32)
    n_valid = (ends[-1:] // MOE_BLOCK).astype(jnp.int32)
    e1 = info[:, _R_E1].astype(jnp.int32)
    e2 = info[:, _R_E2].astype(jnp.int32)
    slot = jnp.concatenate([starts[e1] + info[:, _R_RANK1].astype(jnp.int32),
                            starts[e2] + info[:, _R_RANK2].astype(jnp.int32)])
    n_asg = n * TOP_K
    p = n_blocks * MOE_BLOCK
    asg_of_slot = jnp.full((p,), -1, jnp.int32).at[slot].set(jnp.arange(n_asg, dtype=jnp.int32))
    is_pad = asg_of_slot < 0
    slot_tok = jnp.where(is_pad, 0, asg_of_slot % n)
    slot_dst = jnp.where(is_pad, n_asg - 1 + jnp.cumsum(is_pad.astype(jnp.int32)), asg_of_slot)
    ys = _moe_ffn(block_e, n_valid, slot_tok, slot_dst, xf, w1, w3, w2, j)
    return _combine(h2, info, ys, final_g)


def _row3(a):
    return a.reshape(a.shape[0], 1, a.shape[1])


def kernel(x, mem, mem_norm_g, norm_mix_g, w_in, b_in, conv_dw_w, conv_dw_b, conv_ln_g, conv_ln_b, w_conv_out,
           pool_w, pool_scale, w_pool_out, mlstm_conv_w, mlstm_conv_b, mlstm_norm_g, w_mlstm_out, w_out,
           norm_xattn_g, xa_wq, xa_wk, xa_wv, xa_wo, norm_ffn_g, ffn_w1, ffn_w3, ffn_w2, router_w,
           moe_w1, moe_w3, moe_w2, final_norm_g):
    bsz, s, d = x.shape
    n = bsz * s
    depth = w_in.shape[0]
    bf = lambda a: a.astype(BF16)

    def reorder(a):
        gap = ((0, 0), (0, 0), (0, _W_G01 - _W_IF - 2 * HEADS))
        return jnp.concatenate([a[:, :, :_M_IF], jnp.pad(a[:, :, _M_IF:_G_PRE], gap), a[:, :, _G_PRE:]], axis=2)

    w_all = bf(reorder(w_in))
    b_all = reorder(_row3(b_in))
    gmix = _row3(norm_mix_g)
    gxa = _row3(norm_xattn_g)
    gffn = _row3(norm_ffn_g)
    final_g = final_norm_g.reshape(1, d)
    conv_b3, ln_g3, ln_b3 = _row3(conv_dw_b), _row3(conv_ln_g), _row3(conv_ln_b)
    w_co, w_po, pool_wb = bf(w_conv_out), bf(w_pool_out), bf(pool_w)
    pool_s3 = _row3(pool_scale)
    mconv_b3, mnorm_g3 = _row3(mlstm_conv_b), _row3(mlstm_norm_g)
    w_mo, w_o = bf(w_mlstm_out), bf(w_out)
    wq, wk, wv, wo = bf(xa_wq), bf(xa_wk), bf(xa_wv), bf(xa_wo)
    f1, f3, f2 = bf(ffn_w1), bf(ffn_w3), bf(ffn_w2)
    m1, m3, m2 = bf(moe_w1), bf(moe_w3), bf(moe_w2)
    w_router = jnp.pad(router_w, ((0, 0), (0, 0), (0, LANES - N_EXPERTS)))
    w_router_hi = bf(w_router)
    w_router_hl = jnp.concatenate([w_router_hi, bf(w_router - w_router_hi.astype(F32))], axis=2)

    kmem, vmem = _memkv(mem, mem_norm_g.reshape(1, d), wk, wv)

    h = x
    for l in range(depth):
        last = l == depth - 1
        cv, pu, q, k, v, z, gif = _inproj(h, gmix, w_all, b_all, conv_dw_w, conv_b3,
                                          mlstm_conv_w, mconv_b3, l)
        yab = _branches(h, gmix, cv, pu, ln_g3, ln_b3, w_co, pool_wb, pool_s3, w_po, w_all, b_all, l)
        hz = _mlstm(q, k, v, z, gif, mnorm_g3, l)
        j = l // 2
        if l % 2 == 0:
            (h2,) = _merge_xattn(h, gmix, yab, hz, w_all, b_all, w_mo, w_o, gxa, wq, kmem, vmem, wo, l)
            hnext = _ffn(h2.reshape(n, d), gffn, f1, f3, f2, l, j, final_g if last else None)
        else:
            h2, xf, logits = _merge_xattn(h, gmix, yab, hz, w_all, b_all, w_mo, w_o, gxa, wq, kmem, vmem, wo, l,
                                          router=(gffn, w_router_hl, j))
            hnext = _moe(h2.reshape(n, d), xf.reshape(n, d), logits.reshape(n, LANES), m1, m3, m2, j,
                         final_g if last else None)
        h = hnext.reshape(bsz, s, d)
    return h
```

```python
import functools

import jax
import jax.numpy as jnp
from jax import lax
from jax.experimental import pallas as pl
from jax.experimental.pallas import tpu as pltpu

F32 = jnp.float32
BF16 = jnp.bfloat16
EPS = 1e-6
NEG = -1e30

D_MODEL = 1024
CONV_W = 512
CONV_K = 31
POOL_W = 512
POOL_GW = 128
POOL_WINDOWS = (2, 4, 8, 16)
MLSTM_W = 1024
HEADS = 4
DH = 256
MLSTM_CONV_K = 4
XA_HEADS = 4
XA_DH = 256
N_EXPERTS = 8
TOP_K = 2

LANES = 128
SUBLANES = 8
VMEM_LIMIT = 56 * 1024 * 1024

ROW_TILE = 512
IN_HALO = 32
POOL_HALO = 16
QK_HALO = 16
VPU_ROWS = 64
MLSTM_CHUNK = 512
MOE_BLOCK = 512
FF_CHUNK_DENSE = 1408
FF_CHUNK_MOE = 512
ROUTE_TILE = 512
DMA_UNROLL = 8

_C_A = 0
_C_B = _C_A + CONV_W
_P_U = _C_B + CONV_W
_M_QK = _P_U + POOL_W
_M_V = _M_QK + 2 * MLSTM_W
_M_Z = _M_V + MLSTM_W
_M_IF = _M_Z + MLSTM_W
_G_PRE = _M_IF + 2 * HEADS
_W_IF = _M_IF
_W_G01 = -(-(_W_IF + LANES) // (2 * D_MODEL)) * (2 * D_MODEL)
_W_G2 = _W_G01 + 2 * D_MODEL


def _params(n_axes=1):
    return pltpu.CompilerParams(dimension_semantics=("arbitrary",) * n_axes, vmem_limit_bytes=VMEM_LIMIT)


def _resident(block_shape, index_map):
    return pl.BlockSpec(block_shape, index_map, pipeline_mode=pl.Buffered(1))


def _rms(x, g):
    return x * lax.rsqrt(jnp.mean(x * x, axis=-1, keepdims=True) + EPS) * g


def _sigmoid(x):
    return 1.0 / (1.0 + jnp.exp(-x))


def _dot(a, b):
    return jnp.dot(a, b, preferred_element_type=F32)


def _inproj_body(h_ref, hprev_ref, g_ref, w_ref, b_ref, wif_ref, bif_ref, cw_ref, cb_ref, mw_ref, mb_ref,
                 cv_ref, pu_ref, q_ref, k_ref, v_ref, z_ref, if_ref,
                 xn_ext, u_ext, shifted, qk_ext):
    t = h_ref.shape[0]
    has_prev = pl.program_id(1) > 0

    xn_ext[0:IN_HALO, :] = _rms(hprev_ref[...], g_ref[...]).astype(BF16)
    xn_ext[IN_HALO:IN_HALO + t, :] = _rms(h_ref[...], g_ref[...]).astype(BF16)

    def proj(x, lo, hi):
        return _dot(x, w_ref[:, lo:hi]) + b_ref[:, lo:hi]

    def causal_input(y, halo):
        row = lax.broadcasted_iota(jnp.int32, y.shape, 0)
        return jnp.where(jnp.logical_or(has_prev, row >= halo), y, 0.0)

    def conv31_lanes(c0):
        base = IN_HALO - (CONV_K - 1)
        for phase in range(SUBLANES):
            rows = t + (CONV_K - 1 - phase) // SUBLANES * SUBLANES
            shifted[c0 // LANES % 2, phase, 0:rows, :] = u_ext[base + phase:base + phase + rows, c0:c0 + LANES]
        for r0 in range(0, t, VPU_ROWS):
            acc = jnp.broadcast_to(cb_ref[:, c0:c0 + LANES], (VPU_ROWS, LANES))
            for j in range(CONV_K):
                phase, off = j % SUBLANES, j // SUBLANES * SUBLANES
                acc = acc + cw_ref[j:j + 1, c0:c0 + LANES] * shifted[c0 // LANES % 2, phase, r0 + off:r0 + off + VPU_ROWS, :]
            cv_ref[r0:r0 + VPU_ROWS, c0:c0 + LANES] = acc

    def qk_project(half):
        lo = _M_QK + half * MLSTM_W
        xq = xn_ext[IN_HALO - QK_HALO:IN_HALO + t, :]
        qk_ext[half] = causal_input(proj(xq, lo, lo + MLSTM_W), QK_HALO)

    def qk_conv(half, out_ref):
        base = QK_HALO - (MLSTM_CONV_K - 1)
        cols = slice(half * MLSTM_W, (half + 1) * MLSTM_W)
        conv = jnp.broadcast_to(mb_ref[:, cols], (t, MLSTM_W))
        for j in range(MLSTM_CONV_K):
            conv = conv + mw_ref[j:j + 1, cols] * qk_ext[half, base + j:base + j + t, :]
        out_ref[...] = (conv * _sigmoid(conv)).astype(out_ref.dtype)

    xe = xn_ext[...]
    xn = xn_ext[IN_HALO:IN_HALO + t, :]
    u_ext[...] = causal_input(proj(xe, _C_A, _C_B) * _sigmoid(proj(xe, _C_B, _P_U)), IN_HALO)
    qk_project(0)
    conv31_lanes(0 * LANES)
    qk_project(1)
    qk_conv(0, q_ref)
    pu_ref[...] = proj(xn, _P_U, _M_QK)
    conv31_lanes(1 * LANES)
    v_ref[...] = proj(xn, _M_V, _M_Z).astype(BF16)
    qk_conv(1, k_ref)
    z_ref[...] = proj(xn, _M_Z, _M_IF)
    conv31_lanes(2 * LANES)
    if_ref[...] = _dot(xn, wif_ref[...]) + bif_ref[...]
    conv31_lanes(3 * LANES)


def _inproj(h, gmix, w_all, b_all, conv_w, conv_b, mconv_w, mconv_b, l):
    bsz, s, _ = h.shape
    t = ROW_TILE
    lsel = lambda b, i: (l, 0, 0)
    cur = lambda b, i: (b, i, 0)
    prev = lambda b, i: (b, jnp.maximum(i * (t // IN_HALO) - 1, 0), 0)
    widths = (CONV_W, POOL_W, MLSTM_W, MLSTM_W, MLSTM_W, MLSTM_W, LANES)
    dtypes = (F32, F32, BF16, F32, BF16, F32, F32)
    return pl.pallas_call(
        _inproj_body,
        grid=(bsz, s // t),
        in_specs=[
            pl.BlockSpec((None, t, D_MODEL), cur),
            pl.BlockSpec((None, IN_HALO, D_MODEL), prev),
            _resident((None, 1, D_MODEL), lsel),
            _resident((None, D_MODEL, _M_IF), lsel),
            _resident((None, 1, _M_IF), lsel),
            _resident((None, D_MODEL, LANES), lambda b, i: (l, 0, _W_IF // LANES)),
            _resident((None, 1, LANES), lambda b, i: (l, 0, _W_IF // LANES)),
            _resident((None, CONV_K, CONV_W), lsel),
            _resident((None, 1, CONV_W), lsel),
            _resident((None, MLSTM_CONV_K, 2 * MLSTM_W), lsel),
            _resident((None, 1, 2 * MLSTM_W), lsel),
        ],
        out_specs=[pl.BlockSpec((None, t, w), cur) for w in widths],
        out_shape=[jax.ShapeDtypeStruct((bsz, s, w), dt) for w, dt in zip(widths, dtypes)],
        scratch_shapes=[
            pltpu.VMEM((IN_HALO + t, D_MODEL), BF16),
            pltpu.VMEM((IN_HALO + t, CONV_W), F32),
            pltpu.VMEM((2, SUBLANES, t + (CONV_K - 1) // SUBLANES * SUBLANES, LANES), F32),
            pltpu.VMEM((2, QK_HALO + t, MLSTM_W), F32),
        ],
        compiler_params=_params(2),
        name="inproj",
    )(h, h, gmix, w_all, b_all, w_all, b_all, conv_w, conv_b, mconv_w, mconv_b)


def _branches_body(h_ref, g_ref, cv_ref, pu_ref, puprev_ref,
                   lng_ref, lnb_ref, wco_ref,
                   pw_ref, ps_ref, wpo_ref, wg_ref, bg_ref,
                   yab_ref, extp, pools):
    t = pu_ref.shape[0]
    i = pl.program_id(1)
    has_prev = i > 0

    cv = cv_ref[...]
    mu = jnp.mean(cv, axis=-1, keepdims=True)
    var = jnp.mean(jnp.square(cv - mu), axis=-1, keepdims=True)
    a = (cv - mu) * lax.rsqrt(var + EPS) * lng_ref[...] + lnb_ref[...]
    a = a * _sigmoid(a)
    ya = _dot(a.astype(BF16), wco_ref[...])

    extp[0:POOL_HALO, :] = jnp.where(has_prev, puprev_ref[...], 0.0)
    extp[POOL_HALO:POOL_HALO + t, :] = pu_ref[...]
    for g, window in enumerate(POOL_WINDOWS):
        c0 = g * POOL_GW
        for r0 in range(0, t, VPU_ROWS):
            own = extp[POOL_HALO + r0:POOL_HALO + r0 + VPU_ROWS, c0:c0 + POOL_GW]
            acc = own
            for k in range(1, window):
                acc = acc + extp[POOL_HALO + r0 - k:POOL_HALO + r0 - k + VPU_ROWS, c0:c0 + POOL_GW]
            pos = i * t + r0 + lax.broadcasted_iota(jnp.int32, (VPU_ROWS, POOL_GW), 0)
            cnt = jnp.minimum(pos + 1, window).astype(F32)
            pools[r0:r0 + VPU_ROWS, c0:c0 + POOL_GW] = acc / cnt - own
    yp = jnp.concatenate(
        [_dot(pools[:, g * POOL_GW:(g + 1) * POOL_GW].astype(BF16), pw_ref[g]) for g in range(len(POOL_WINDOWS))],
        axis=1) * ps_ref[...]
    yb = _dot(yp.astype(BF16), wpo_ref[...])

    xn = _rms(h_ref[...], g_ref[...]).astype(BF16)
    gates = _sigmoid(_dot(xn, wg_ref[...]) + bg_ref[...])
    yab_ref[...] = gates[:, :D_MODEL] * ya + gates[:, D_MODEL:] * yb


def _branches(h, gmix, cv, pu, ln_g, ln_b, w_conv_out, pool_w, pool_scale, w_pool_out, w_gate, b_gate, l):
    bsz, s, _ = h.shape
    t = ROW_TILE
    lsel3 = lambda b, i: (l, 0, 0)
    lsel4 = lambda b, i: (l, 0, 0, 0)
    cur = lambda b, i: (b, i, 0)
    prev_p = lambda b, i: (b, jnp.maximum(i * (t // POOL_HALO) - 1, 0), 0)
    return pl.pallas_call(
        _branches_body,
        grid=(bsz, s // t),
        in_specs=[
            pl.BlockSpec((None, t, D_MODEL), cur),
            _resident((None, 1, D_MODEL), lsel3),
            pl.BlockSpec((None, t, CONV_W), cur),
            pl.BlockSpec((None, t, POOL_W), cur),
            pl.BlockSpec((None, POOL_HALO, POOL_W), prev_p),
            _resident((None, 1, CONV_W), lsel3),
            _resident((None, 1, CONV_W), lsel3),
            _resident((None, CONV_W, D_MODEL), lsel3),
            _resident((None, len(POOL_WINDOWS), POOL_GW, POOL_GW), lsel4),
            _resident((None, 1, POOL_W), lsel3),
            _resident((None, POOL_W, D_MODEL), lsel3),
            _resident((None, D_MODEL, 2 * D_MODEL), lambda b, i: (l, 0, _W_G01 // (2 * D_MODEL))),
            _resident((None, 1, 2 * D_MODEL), lambda b, i: (l, 0, _W_G01 // (2 * D_MODEL))),
        ],
        out_specs=pl.BlockSpec((None, t, D_MODEL), cur),
        out_shape=jax.ShapeDtypeStruct((bsz, s, D_MODEL), F32),
        scratch_shapes=[
            pltpu.VMEM((POOL_HALO + t, POOL_W), F32),
            pltpu.VMEM((t, POOL_W), F32),
        ],
        compiler_params=_params(2),
        name="branches",
    )(h, gmix, cv, pu, pu, ln_g, ln_b, w_conv_out, pool_w, pool_scale, w_pool_out, w_gate, b_gate)


def _cumsum_rows(x):
    n = x.shape[0]
    row = lax.broadcasted_iota(jnp.int32, x.shape, 0)
    shift = 1
    while shift < n:
        x = x + jnp.where(row >= shift, pltpu.roll(x, shift, axis=0), 0.0)
        shift *= 2
    return x


def _mlstm_body(q_ref, k_ref, v_ref, z_ref, if_ref, ng_ref, hz_ref, ct_scr, n_scr, m_scr):
    ln = q_ref.shape[0]
    c = pl.program_id(1)

    @pl.when(c == 0)
    def _():
        ct_scr[...] = jnp.zeros_like(ct_scr)
        n_scr[...] = jnp.zeros_like(n_scr)
        m_scr[...] = jnp.zeros_like(m_scr)

    gates = if_ref[...]
    logf = jnp.minimum(gates, 0.0) - jnp.log(1.0 + jnp.exp(-jnp.abs(gates)))
    bcum = _cumsum_rows(logf)
    lane = lax.broadcasted_iota(jnp.int32, gates.shape, 1)
    rows_t = jnp.where(lane < HEADS, gates, bcum).T

    tt = lax.broadcasted_iota(jnp.int32, (ln, ln), 0)
    ss = lax.broadcasted_iota(jnp.int32, (ln, ln), 1)
    causal = ss <= tt

    for hd in range(HEADS):
        qb = q_ref[:, hd * DH:(hd + 1) * DH]
        q = qb.astype(F32)
        k = k_ref[:, hd * DH:(hd + 1) * DH] * (DH ** -0.5)
        v = v_ref[:, hd * DH:(hd + 1) * DH]
        i_col = gates[:, hd:hd + 1]
        b_col = bcum[:, HEADS + hd:HEADS + hd + 1]
        i_row = rows_t[hd:hd + 1, :]
        b_row = rows_t[HEADS + hd:HEADS + hd + 1, :]
        m_prev = m_scr[hd][:, 0:1]
        ct = ct_scr[hd]
        nvec = n_scr[hd]

        log_d = jnp.where(causal, b_col - b_row + i_row, NEG)
        inter = b_col + m_prev
        mt = jnp.maximum(inter, jnp.max(log_d, axis=-1, keepdims=True))
        dmat = jnp.exp(log_d - mt)
        a_inter = jnp.exp(inter - mt)
        kb = k.astype(BF16)
        sc = lax.dot_general(qb, kb, (((1,), (1,)), ((), ())), preferred_element_type=F32) * dmat
        num = _dot(sc.astype(BF16), v) + a_inter * _dot(qb, ct.astype(BF16))
        den = jnp.sum(sc, axis=-1, keepdims=True) + a_inter * jnp.sum(q * nvec, axis=-1, keepdims=True)
        hh = num / jnp.maximum(jnp.abs(den), jnp.exp(-mt))

        b_last = b_col[ln - 1:ln, :]
        w_log = b_last - b_col + i_col
        m_new = jnp.maximum(b_last + m_prev, jnp.max(w_log, axis=0, keepdims=True))
        a_c = jnp.exp(b_last + m_prev - m_new)
        kw = k * jnp.exp(w_log - m_new)
        ct_scr[hd] = a_c * ct + lax.dot_general(kw.astype(BF16), v, (((0,), (0,)), ((), ())),
                                                preferred_element_type=F32)
        n_scr[hd] = a_c * nvec + jnp.sum(kw, axis=0, keepdims=True)
        m_scr[hd] = jnp.broadcast_to(m_new, (1, LANES))

        hn = hh * lax.rsqrt(jnp.mean(hh * hh, axis=-1, keepdims=True) + EPS) * ng_ref[:, hd * DH:(hd + 1) * DH]
        hz_ref[:, hd * DH:(hd + 1) * DH] = (hn * _sigmoid(z_ref[:, hd * DH:(hd + 1) * DH])).astype(BF16)


def _mlstm(q, k, v, z, gates_if, norm_g, l):
    bsz, s, _ = q.shape
    ln = MLSTM_CHUNK
    lsel3 = lambda b, c: (l, 0, 0)
    cur = lambda b, c: (b, c, 0)
    wide = pl.BlockSpec((None, ln, MLSTM_W), cur)
    return pl.pallas_call(
        _mlstm_body,
        grid=(bsz, s // ln),
        in_specs=[wide, wide, wide, wide, pl.BlockSpec((None, ln, LANES), cur), _resident((None, 1, MLSTM_W), lsel3)],
        out_specs=wide,
        out_shape=jax.ShapeDtypeStruct((bsz, s, MLSTM_W), BF16),
        scratch_shapes=[
            pltpu.VMEM((HEADS, DH, DH), F32),
            pltpu.VMEM((HEADS, 1, DH), F32),
            pltpu.VMEM((HEADS, 1, LANES), F32),
        ],
        compiler_params=_params(2),
        name="mlstm",
    )(q, k, v, z, gates_if, norm_g)


def _merge_xattn_body(with_router, h_ref, gmix_ref, yab_ref, hz_ref, wg_ref, bg_ref, wmo_ref, wout_ref,
                      gxa_ref, wq_ref, k_ref, v_ref, wo_ref, *rest):
    if with_router:
        gffn_ref, wr_ref, h2_ref, xf_ref, lg_ref = rest
    else:
        (h2_ref,) = rest
    h = h_ref[...]
    xn = _rms(h, gmix_ref[...]).astype(BF16)
    g2 = _sigmoid(_dot(xn, wg_ref[...]) + bg_ref[...])
    y = yab_ref[...] + g2 * _dot(hz_ref[...], wmo_ref[...])
    h1 = h + _dot(y.astype(BF16), wout_ref[...])

    hn = _rms(h1, gxa_ref[...]).astype(BF16)
    q = _dot(hn, wq_ref[...])
    heads = []
    for hd in range(XA_HEADS):
        sl = slice(hd * XA_DH, (hd + 1) * XA_DH)
        sc = lax.dot_general(q[:, sl].astype(BF16), k_ref[:, sl], (((1,), (1,)), ((), ())),
                             preferred_element_type=F32) * (XA_DH ** -0.5)
        e = jnp.exp(sc - jnp.max(sc, axis=-1, keepdims=True))
        p = e / jnp.sum(e, axis=-1, keepdims=True)
        heads.append(_dot(p.astype(BF16), v_ref[:, sl]))
    o = jnp.concatenate(heads, axis=1)
    h2 = h1 + _dot(o.astype(BF16), wo_ref[...])
    h2_ref[...] = h2
    if with_router:
        xf = _rms(h2, gffn_ref[...])
        xf_ref[...] = xf
        t = xf.shape[0]
        xf_hi = xf.astype(BF16)
        xf_lo = (xf - xf_hi.astype(F32)).astype(BF16)
        r = _dot(jnp.concatenate([xf_hi, xf_lo], axis=0), wr_ref[...])
        lg_ref[...] = r[:t, :LANES] + (r[t:, :LANES] + r[:t, LANES:])


def _merge_xattn(h, gmix, yab, hz, w_all, b_all, w_mo, w_out, gxa, wq, kmem, vmem, wo, l,
                 router=None):
    bsz, s, _ = h.shape
    t = ROW_TILE
    m = kmem.shape[2]
    lsel3 = lambda b, i: (l, 0, 0)
    cur = lambda b, i: (b, i, 0)
    mem = lambda b, i: (l, b, 0, 0)
    gsel = lambda b, i: (l, 0, _W_G2 // D_MODEL)
    sq = (None, D_MODEL, D_MODEL)
    in_specs = [
        pl.BlockSpec((None, t, D_MODEL), cur),
        _resident((None, 1, D_MODEL), lsel3),
        pl.BlockSpec((None, t, D_MODEL), cur),
        pl.BlockSpec((None, t, D_MODEL), cur),
        _resident(sq, gsel),
        _resident((None, 1, D_MODEL), gsel),
        _resident(sq, lsel3),
        _resident(sq, lsel3),
        _resident((None, 1, D_MODEL), lsel3),
        _resident(sq, lsel3),
        pl.BlockSpec((None, None, m, D_MODEL), mem),
        pl.BlockSpec((None, None, m, D_MODEL), mem),
        _resident(sq, lsel3),
    ]
    args = [h, gmix, yab, hz, w_all, b_all, w_mo, w_out, gxa, wq, kmem, vmem, wo]
    out_specs = [pl.BlockSpec((None, t, D_MODEL), cur)]
    out_shape = [jax.ShapeDtypeStruct((bsz, s, D_MODEL), F32)]
    if router is not None:
        gffn, w_router, j = router
        in_specs += [_resident((None, 1, D_MODEL), lsel3),
                     _resident((None, D_MODEL, 2 * LANES), lambda b, i: (j, 0, 0))]
        args += [gffn, w_router]
        out_specs += [pl.BlockSpec((None, t, D_MODEL), cur), pl.BlockSpec((None, t, LANES), cur)]
        out_shape += [jax.ShapeDtypeStruct((bsz, s, D_MODEL), F32), jax.ShapeDtypeStruct((bsz, s, LANES), F32)]
    return pl.pallas_call(
        functools.partial(_merge_xattn_body, router is not None),
        grid=(bsz, s // t),
        in_specs=in_specs,
        out_specs=out_specs,
        out_shape=out_shape,
        compiler_params=_params(2),
        name="merge_xattn_router" if router is not None else "merge_xattn",
    )(*args)


def _memkv_body(mem_ref, g_ref, wk_ref, wv_ref, k_ref, v_ref):
    mn = _rms(mem_ref[...], g_ref[...]).astype(BF16)
    k_ref[...] = _dot(mn, wk_ref[...]).astype(BF16)
    v_ref[...] = _dot(mn, wv_ref[...]).astype(BF16)


def _memkv(mem, g, wk, wv):
    bsz, m, _ = mem.shape
    depth = wk.shape[0]
    out = jax.ShapeDtypeStruct((depth, bsz, m, D_MODEL), BF16)
    return pl.pallas_call(
        _memkv_body,
        grid=(depth, bsz),
        in_specs=[
            pl.BlockSpec((None, m, D_MODEL), lambda l, b: (b, 0, 0)),
            pl.BlockSpec((1, D_MODEL), lambda l, b: (0, 0)),
            pl.BlockSpec((None, D_MODEL, D_MODEL), lambda l, b: (l, 0, 0)),
            pl.BlockSpec((None, D_MODEL, D_MODEL), lambda l, b: (l, 0, 0)),
        ],
        out_specs=[pl.BlockSpec((None, None, m, D_MODEL), lambda l, b: (l, b, 0, 0))] * 2,
        out_shape=[out, out],
        compiler_params=_params(2),
        name="memkv",
    )(mem, g, wk, wv)


def _ffn_body(final_norm, h_ref, g_ref, w1_ref, w3_ref, w2_ref, *rest):
    if final_norm:
        gfin_ref, out_ref = rest
    else:
        (out_ref,) = rest
    h = h_ref[...]
    xn = _rms(h, g_ref[...]).astype(BF16)
    acc = h
    ff = w1_ref.shape[1]
    for f0 in range(0, ff, FF_CHUNK_DENSE):
        a = _dot(xn, w1_ref[:, f0:f0 + FF_CHUNK_DENSE])
        b = _dot(xn, w3_ref[:, f0:f0 + FF_CHUNK_DENSE])
        acc = acc + _dot((a * _sigmoid(a) * b).astype(BF16), w2_ref[f0:f0 + FF_CHUNK_DENSE, :])
    out_ref[...] = _rms(acc, gfin_ref[...]) if final_norm else acc


def _ffn(h, gffn, w1, w3, w2, l, j, final_g=None):
    n = h.shape[0]
    t = ROW_TILE
    ff = w1.shape[2]
    row = lambda i: (i, 0)
    in_specs = [
        pl.BlockSpec((t, D_MODEL), row),
        _resident((None, 1, D_MODEL), lambda i: (l, 0, 0)),
        _resident((None, D_MODEL, ff), lambda i: (j, 0, 0)),
        _resident((None, D_MODEL, ff), lambda i: (j, 0, 0)),
        _resident((None, ff, D_MODEL), lambda i: (j, 0, 0)),
    ]
    args = [h, gffn, w1, w3, w2]
    if final_g is not None:
        in_specs.append(_resident((1, D_MODEL), lambda i: (0, 0)))
        args.append(final_g)
    return pl.pallas_call(
        functools.partial(_ffn_body, final_g is not None),
        grid=(n // t,),
        in_specs=in_specs,
        out_specs=pl.BlockSpec((t, D_MODEL), row),
        out_shape=jax.ShapeDtypeStruct((n, D_MODEL), F32),
        compiler_params=_params(1),
        name="ffn_dense",
    )(*args)


_R_E1, _R_E2, _R_G1, _R_G2, _R_RANK1, _R_RANK2 = range(6)


def _route_body(lg_ref, info_ref, cnt_ref, carry):
    t = lg_ref.shape[0]
    i = pl.program_id(0)

    @pl.when(i == 0)
    def _():
        carry[...] = jnp.zeros_like(carry)

    lane = lax.broadcasted_iota(jnp.int32, (t, LANES), 1)
    lanef = lane.astype(F32)
    lg = jnp.where(lane < N_EXPERTS, lg_ref[...], NEG)
    m1 = jnp.max(lg, axis=-1, keepdims=True)
    e1 = jnp.min(jnp.where(lg == m1, lanef, float(LANES)), axis=-1, keepdims=True)
    sel1 = lanef == e1
    lg2 = jnp.where(sel1, NEG, lg)
    m2 = jnp.max(lg2, axis=-1, keepdims=True)
    e2 = jnp.min(jnp.where(lg2 == m2, lanef, float(LANES)), axis=-1, keepdims=True)
    sel2 = lanef == e2
    ex = jnp.exp(m2 - m1)
    g1 = 1.0 / (1.0 + ex)
    g2 = ex / (1.0 + ex)

    onehot = jnp.where(sel1 | sel2, 1.0, 0.0)
    rr = lax.broadcasted_iota(jnp.int32, (t, t), 0)
    cc = lax.broadcasted_iota(jnp.int32, (t, t), 1)
    before = jnp.where(cc < rr, 1.0, 0.0).astype(BF16)
    pref = _dot(before, onehot.astype(BF16)) + carry[...]
    r1 = jnp.sum(jnp.where(sel1, pref, 0.0), axis=-1, keepdims=True)
    r2 = jnp.sum(jnp.where(sel2, pref, 0.0), axis=-1, keepdims=True)
    carry[...] = carry[...] + jnp.sum(onehot, axis=0, keepdims=True)

    info = jnp.zeros((t, LANES), F32)
    for col, val in ((_R_E1, e1), (_R_E2, e2), (_R_G1, g1), (_R_G2, g2), (_R_RANK1, r1), (_R_RANK2, r2)):
        info = jnp.where(lane == col, val, info)
    info_ref[...] = info
    cnt_ref[...] = carry[...]


def _route(logits):
    n = logits.shape[0]
    t = ROUTE_TILE
    return pl.pallas_call(
        _route_body,
        grid=(n // t,),
        in_specs=[pl.BlockSpec((t, LANES), lambda i: (i, 0))],
        out_specs=[pl.BlockSpec((t, LANES), lambda i: (i, 0)), pl.BlockSpec((1, LANES), lambda i: (0, 0))],
        out_shape=[jax.ShapeDtypeStruct((n, LANES), F32), jax.ShapeDtypeStruct((1, LANES), F32)],
        scratch_shapes=[pltpu.VMEM((1, LANES), F32)],
        compiler_params=_params(1),
        name="moe_route",
    )(logits)


def _moe_ffn_body(j, be_ref, nv_ref, stok_ref, sdst_ref, xf_hbm, w1_hbm, w3_hbm, w2_hbm, ys_hbm,
                  xbuf, ybuf, wb1, wb3, wb2, stage1, stage3, stage2, gsem, ssem, wsem):
    b = pl.program_id(0)
    last_blk = pl.num_programs(0) - 1
    nv = nv_ref[0]
    cur = b % 2
    nxt = 1 - cur
    ff = wb1.shape[1]
    n_chunks = ff // FF_CHUNK_MOE

    def gather_row(blk, buf, r):
        return pltpu.make_async_copy(xf_hbm.at[pl.ds(stok_ref[blk * MOE_BLOCK + r], 1)],
                                     xbuf.at[buf, pl.ds(r, 1)], gsem.at[buf])

    def scatter_row(blk, buf, r):
        return pltpu.make_async_copy(ybuf.at[buf, pl.ds(r, 1)],
                                     ys_hbm.at[pl.ds(sdst_ref[(blk + 1) * MOE_BLOCK + r], 1)], ssem.at[buf])

    def rolled(fn):
        def body(r, carry):
            fn(r)
            return carry
        lax.fori_loop(0, MOE_BLOCK, body, 0, unroll=DMA_UNROLL)

    def inline(fn, rows=range(MOE_BLOCK)):
        for r in rows:
            fn(r)

    @pl.when(b == 0)
    def _():
        rolled(lambda r: gather_row(0, 0, r).start())
        ybuf[1] = jnp.zeros((MOE_BLOCK, D_MODEL), F32)

    @pl.when(b < nv)
    def _():
        expert = be_ref[b]
        new_expert = jnp.logical_or(b == 0, expert != be_ref[jnp.maximum(b - 1, 0)])

        def weight_chunk(f, op):
            cols = pl.ds(f * FF_CHUNK_MOE, FF_CHUNK_MOE)
            slot = f % 2
            for src, dst in ((w1_hbm.at[j, expert, :, cols], stage1.at[slot]),
                             (w3_hbm.at[j, expert, :, cols], stage3.at[slot]),
                             (w2_hbm.at[j, expert, cols, :], stage2.at[slot])):
                op(pltpu.make_async_copy(src, dst, wsem.at[slot]))

        inline(lambda r: gather_row(b, cur, r).wait())

        @pl.when(b >= 1)
        def _():
            inline(lambda r: scatter_row(b - 2, cur, r).wait())

        @pl.when(new_expert)
        def _():
            weight_chunk(0, lambda c: c.start())

        x = xbuf[cur].astype(BF16)
        ahead = jnp.minimum(b + 1, last_blk)
        per_chunk = -(-MOE_BLOCK // n_chunks)
        for f in range(n_chunks):
            fcols = slice(f * FF_CHUNK_MOE, (f + 1) * FF_CHUNK_MOE)

            @pl.when(new_expert)
            def _():
                if f + 1 < n_chunks:
                    weight_chunk(f + 1, lambda c: c.start())
                weight_chunk(f, lambda c: c.wait())
                wb1[:, fcols] = stage1[f % 2].astype(BF16)
                wb3[:, fcols] = stage3[f % 2].astype(BF16)
                wb2[fcols, :] = stage2[f % 2].astype(BF16)

            rows = range(f * per_chunk, min((f + 1) * per_chunk, MOE_BLOCK))
            inline(lambda r: gather_row(ahead, nxt, r).start(), rows)
            inline(lambda r: scatter_row(b - 1, nxt, r).start(), rows)
            a = _dot(x, wb1[:, fcols])
            g = _dot(x, wb3[:, fcols])
            part = _dot((a * _sigmoid(a) * g).astype(BF16), wb2[fcols, :])
            if f == 0:
                ybuf[cur] = part
            else:
                ybuf[cur] = ybuf[cur] + part

        @pl.when(b == nv - 1)
        def _():
            rolled(lambda r: gather_row(ahead, nxt, r).wait())
            rolled(lambda r: scatter_row(b, cur, r).start())
            rolled(lambda r: scatter_row(b - 1, nxt, r).wait())
            rolled(lambda r: scatter_row(b, cur, r).wait())

    @pl.when(b >= nv)
    def _():
        ybuf[0] = jnp.zeros((MOE_BLOCK, D_MODEL), F32)
        rolled(lambda r: scatter_row(b, 0, r).start())
        rolled(lambda r: scatter_row(b, 0, r).wait())


def _moe_ffn(block_e, n_valid, slot_tok, slot_dst, xf, w1, w3, w2, j):
    p = slot_tok.shape[0]
    ff = w1.shape[3]
    spare = p + jnp.arange(MOE_BLOCK, dtype=jnp.int32)
    slot_dst = jnp.concatenate([spare, slot_dst])
    any_spec = pl.BlockSpec(memory_space=pl.ANY)
    return pl.pallas_call(
        functools.partial(_moe_ffn_body, j),
        grid_spec=pltpu.PrefetchScalarGridSpec(
            num_scalar_prefetch=4,
            grid=(p // MOE_BLOCK,),
            in_specs=[any_spec] * 4,
            out_specs=any_spec,
            scratch_shapes=[
                pltpu.VMEM((2, MOE_BLOCK, D_MODEL), F32),
                pltpu.VMEM((2, MOE_BLOCK, D_MODEL), F32),
                pltpu.VMEM((D_MODEL, ff), BF16),
                pltpu.VMEM((D_MODEL, ff), BF16),
                pltpu.VMEM((ff, D_MODEL), BF16),
                pltpu.VMEM((2, D_MODEL, FF_CHUNK_MOE), F32),
                pltpu.VMEM((2, D_MODEL, FF_CHUNK_MOE), F32),
                pltpu.VMEM((2, FF_CHUNK_MOE, D_MODEL), F32),
                pltpu.SemaphoreType.DMA((2,)),
                pltpu.SemaphoreType.DMA((2,)),
                pltpu.SemaphoreType.DMA((2,)),
            ],
        ),
        out_shape=jax.ShapeDtypeStruct((p + MOE_BLOCK, D_MODEL), F32),
        compiler_params=pltpu.CompilerParams(dimension_semantics=("arbitrary",), vmem_limit_bytes=VMEM_LIMIT,
                                             has_side_effects=True),
        name="moe_ffn",
    )(block_e, n_valid, slot_tok, slot_dst, xf, w1, w3, w2)


def _combine_body(final_norm, h_ref, info_ref, y1_ref, y2_ref, *rest):
    if final_norm:
        gfin_ref, out_ref = rest
    else:
        (out_ref,) = rest
    info = info_ref[...]
    out = h_ref[...] + (info[:, _R_G1:_R_G1 + 1] * y1_ref[...] + info[:, _R_G2:_R_G2 + 1] * y2_ref[...])
    out_ref[...] = _rms(out, gfin_ref[...]) if final_norm else out


def _combine(h, info, ys, final_g=None):
    n = h.shape[0]
    t = ROW_TILE
    nt = n // t
    row = lambda i: (i, 0)
    in_specs = [
        pl.BlockSpec((t, D_MODEL), row),
        pl.BlockSpec((t, LANES), row),
        pl.BlockSpec((t, D_MODEL), row),
        pl.BlockSpec((t, D_MODEL), lambda i: (i + nt, 0)),
    ]
    args = [h, info, ys, ys]
    if final_g is not None:
        in_specs.append(_resident((1, D_MODEL), lambda i: (0, 0)))
        args.append(final_g)
    return pl.pallas_call(
        functools.partial(_combine_body, final_g is not None),
        grid=(nt,),
        in_specs=in_specs,
        out_specs=pl.BlockSpec((t, D_MODEL), row),
        out_shape=jax.ShapeDtypeStruct((n, D_MODEL), F32),
        compiler_params=_params(1),
        name="moe_combine",
    )(*args)


def _moe(h2, xf, logits, w1, w3, w2, j, final_g=None):
    n = h2.shape[0]
    info, counts = _route(logits)
    cnt = counts[0, :N_EXPERTS].astype(jnp.int32)
    padded = (cnt + MOE_BLOCK - 1) // MOE_BLOCK * MOE_BLOCK
    ends = jnp.cumsum(padded)
    starts = ends - padded
    n_blocks = (n * TOP_K) // MOE_BLOCK + N_EXPERTS
    block_e = jnp.minimum(jnp.searchsorted(ends, jnp.arange(n_blocks, dtype=jnp.int32) * MOE_BLOCK, side="right"),
                          N_EXPERTS - 1).astype(jnp.int32)
    n_valid = (ends[-1:] // MOE_BLOCK).astype(jnp.int32)
    e1 = info[:, _R_E1].astype(jnp.int32)
    e2 = info[:, _R_E2].astype(jnp.int32)
    slot = jnp.concatenate([starts[e1] + info[:, _R_RANK1].astype(jnp.int32),
                            starts[e2] + info[:, _R_RANK2].astype(jnp.int32)])
    n_asg = n * TOP_K
    p = n_blocks * MOE_BLOCK
    asg_of_slot = jnp.full((p,), -1, jnp.int32).at[slot].set(jnp.arange(n_asg, dtype=jnp.int32))
    is_pad = asg_of_slot < 0
    slot_tok = jnp.where(is_pad, 0, asg_of_slot % n)
    slot_dst = jnp.where(is_pad, n_asg - 1 + jnp.cumsum(is_pad.astype(jnp.int32)), asg_of_slot)
    ys = _moe_ffn(block_e, n_valid, slot_tok, slot_dst, xf, w1, w3, w2, j)
    return _combine(h2, info, ys, final_g)


def _row3(a):
    return a.reshape(a.shape[0], 1, a.shape[1])


def kernel(x, mem, mem_norm_g, norm_mix_g, w_in, b_in, conv_dw_w, conv_dw_b, conv_ln_g, conv_ln_b, w_conv_out,
           pool_w, pool_scale, w_pool_out, mlstm_conv_w, mlstm_conv_b, mlstm_norm_g, w_mlstm_out, w_out,
           norm_xattn_g, xa_wq, xa_wk, xa_wv, xa_wo, norm_ffn_g, ffn_w1, ffn_w3, ffn_w2, router_w,
           moe_w1, moe_w3, moe_w2, final_norm_g):
    bsz, s, d = x.shape
    n = bsz * s
    depth = w_in.shape[0]
    bf = lambda a: a.astype(BF16)

    def reorder(a):
        gap = ((0, 0), (0, 0), (0, _W_G01 - _W_IF - 2 * HEADS))
        return jnp.concatenate([a[:, :, :_M_IF], jnp.pad(a[:, :, _M_IF:_G_PRE], gap), a[:, :, _G_PRE:]], axis=2)

    w_all = bf(reorder(w_in))
    b_all = reorder(_row3(b_in))
    gmix = _row3(norm_mix_g)
    gxa = _row3(norm_xattn_g)
    gffn = _row3(norm_ffn_g)
    final_g = final_norm_g.reshape(1, d)
    conv_b3, ln_g3, ln_b3 = _row3(conv_dw_b), _row3(conv_ln_g), _row3(conv_ln_b)
    w_co, w_po, pool_wb = bf(w_conv_out), bf(w_pool_out), bf(pool_w)
    pool_s3 = _row3(pool_scale)
    mconv_b3, mnorm_g3 = _row3(mlstm_conv_b), _row3(mlstm_norm_g)
    w_mo, w_o = bf(w_mlstm_out), bf(w_out)
    wq, wk, wv, wo = bf(xa_wq), bf(xa_wk), bf(xa_wv), bf(xa_wo)
    f1, f3, f2 = bf(ffn_w1), bf(ffn_w3), bf(ffn_w2)
    w_router = jnp.pad(router_w, ((0, 0), (0, 0), (0, LANES - N_EXPERTS)))
    w_router_hi = bf(w_router)
    w_router_hl = jnp.concatenate([w_router_hi, bf(w_router - w_router_hi.astype(F32))], axis=2)

    kmem, vmem = _memkv(mem, mem_norm_g.reshape(1, d), wk, wv)

    h = x
    for l in range(depth):
        last = l == depth - 1
        cv, pu, q, k, v, z, gif = _inproj(h, gmix, w_all, b_all, conv_dw_w, conv_b3,
                                          mlstm_conv_w, mconv_b3, l)
        yab = _branches(h, gmix, cv, pu, ln_g3, ln_b3, w_co, pool_wb, pool_s3, w_po, w_all, b_all, l)
        hz = _mlstm(q, k, v, z, gif, mnorm_g3, l)
        j = l // 2
        if l % 2 == 0:
            (h2,) = _merge_xattn(h, gmix, yab, hz, w_all, b_all, w_mo, w_o, gxa, wq, kmem, vmem, wo, l)
            hnext = _ffn(h2.reshape(n, d), gffn, f1, f3, f2, l, j, final_g if last else None)
        else:
            h2, xf, logits = _merge_xattn(h, gmix, yab, hz, w_all, b_all, w_mo, w_o, gxa, wq, kmem, vmem, wo, l,
                                          router=(gffn, w_router_hl, j))
            hnext = _moe(h2.reshape(n, d), xf.reshape(n, d), logits.reshape(n, LANES), moe_w1, moe_w3, moe_w2, j,
                         final_g if last else None)
        h = hnext.reshape(bsz, s, d)
    return h
```

```python
import functools

import jax
import jax.numpy as jnp
from jax import lax
from jax.experimental import pallas as pl
from jax.experimental.pallas import tpu as pltpu

F32 = jnp.float32
BF16 = jnp.bfloat16
EPS = 1e-6
NEG = -1e30

D_MODEL = 1024
CONV_W = 512
CONV_K = 31
POOL_W = 512
POOL_GW = 128
POOL_WINDOWS = (2, 4, 8, 16)
MLSTM_W = 1024
HEADS = 4
DH = 256
MLSTM_CONV_K = 4
XA_HEADS = 4
XA_DH = 256
N_EXPERTS = 8
TOP_K = 2

LANES = 128
SUBLANES = 8
VMEM_LIMIT = 56 * 1024 * 1024

ROW_TILE = 512
IN_HALO = 32
POOL_HALO = 16
QK_HALO = 16
VPU_ROWS = 64
MLSTM_CHUNK = 512
MOE_BLOCK = 512
FF_CHUNK_DENSE = 1408
FF_CHUNK_MOE = 512
ROUTE_TILE = 512
DMA_UNROLL = 8

_C_A = 0
_C_B = _C_A + CONV_W
_P_U = _C_B + CONV_W
_M_QK = _P_U + POOL_W
_M_V = _M_QK + 2 * MLSTM_W
_M_Z = _M_V + MLSTM_W
_M_IF = _M_Z + MLSTM_W
_G_PRE = _M_IF + 2 * HEADS
_W_IF = _M_IF
_W_G01 = -(-(_W_IF + LANES) // (2 * D_MODEL)) * (2 * D_MODEL)
_W_G2 = _W_G01 + 2 * D_MODEL


def _params(n_axes=1):
    return pltpu.CompilerParams(dimension_semantics=("arbitrary",) * n_axes, vmem_limit_bytes=VMEM_LIMIT)


def _resident(block_shape, index_map):
    return pl.BlockSpec(block_shape, index_map, pipeline_mode=pl.Buffered(1))


def _rms(x, g):
    return x * lax.rsqrt(jnp.mean(x * x, axis=-1, keepdims=True) + EPS) * g


def _sigmoid(x):
    return 1.0 / (1.0 + jnp.exp(-x))


def _dot(a, b):
    return jnp.dot(a, b, preferred_element_type=F32)


def _inproj_body(h_ref, hprev_ref, g_ref, w_ref, b_ref, wif_ref, bif_ref, cw_ref, cb_ref, mw_ref, mb_ref,
                 cv_ref, pu_ref, q_ref, k_ref, v_ref, z_ref, if_ref,
                 xn_ext, u_ext, shifted, qk_ext):
    t = h_ref.shape[0]
    has_prev = pl.program_id(1) > 0

    xn_ext[0:IN_HALO, :] = _rms(hprev_ref[...], g_ref[...]).astype(BF16)
    xn_ext[IN_HALO:IN_HALO + t, :] = _rms(h_ref[...], g_ref[...]).astype(BF16)

    def proj(x, lo, hi):
        return _dot(x, w_ref[:, lo:hi]) + b_ref[:, lo:hi]

    def causal_input(y, halo):
        row = lax.broadcasted_iota(jnp.int32, y.shape, 0)
        return jnp.where(jnp.logical_or(has_prev, row >= halo), y, 0.0)

    def conv31_lanes(c0):
        base = IN_HALO - (CONV_K - 1)
        for phase in range(SUBLANES):
            rows = t + (CONV_K - 1 - phase) // SUBLANES * SUBLANES
            shifted[c0 // LANES % 2, phase, 0:rows, :] = u_ext[base + phase:base + phase + rows, c0:c0 + LANES]
        for r0 in range(0, t, VPU_ROWS):
            acc = jnp.broadcast_to(cb_ref[:, c0:c0 + LANES], (VPU_ROWS, LANES))
            for j in range(CONV_K):
                phase, off = j % SUBLANES, j // SUBLANES * SUBLANES
                acc = acc + cw_ref[j:j + 1, c0:c0 + LANES] * shifted[c0 // LANES % 2, phase, r0 + off:r0 + off + VPU_ROWS, :]
            cv_ref[r0:r0 + VPU_ROWS, c0:c0 + LANES] = acc

    def qk_project(half):
        lo = _M_QK + half * MLSTM_W
        xq = xn_ext[IN_HALO - QK_HALO:IN_HALO + t, :]
        qk_ext[half] = causal_input(proj(xq, lo, lo + MLSTM_W), QK_HALO)

    def qk_conv(half, out_ref):
        base = QK_HALO - (MLSTM_CONV_K - 1)
        cols = slice(half * MLSTM_W, (half + 1) * MLSTM_W)
        conv = jnp.broadcast_to(mb_ref[:, cols], (t, MLSTM_W))
        for j in range(MLSTM_CONV_K):
            conv = conv + mw_ref[j:j + 1, cols] * qk_ext[half, base + j:base + j + t, :]
        out_ref[...] = (conv * _sigmoid(conv)).astype(out_ref.dtype)

    xe = xn_ext[...]
    xn = xn_ext[IN_HALO:IN_HALO + t, :]
    u_ext[...] = causal_input(proj(xe, _C_A, _C_B) * _sigmoid(proj(xe, _C_B, _P_U)), IN_HALO)
    qk_project(0)
    conv31_lanes(0 * LANES)
    qk_project(1)
    qk_conv(0, q_ref)
    pu_ref[...] = proj(xn, _P_U, _M_QK)
    conv31_lanes(1 * LANES)
    v_ref[...] = proj(xn, _M_V, _M_Z).astype(BF16)
    qk_conv(1, k_ref)
    z_ref[...] = proj(xn, _M_Z, _M_IF)
    conv31_lanes(2 * LANES)
    if_ref[...] = _dot(xn, wif_ref[...]) + bif_ref[...]
    conv31_lanes(3 * LANES)


def _inproj(h, gmix, w_all, b_all, conv_w, conv_b, mconv_w, mconv_b, l):
    bsz, s, _ = h.shape
    t = ROW_TILE
    lsel = lambda b, i: (l, 0, 0)
    cur = lambda b, i: (b, i, 0)
    prev = lambda b, i: (b, jnp.maximum(i * (t // IN_HALO) - 1, 0), 0)
    widths = (CONV_W, POOL_W, MLSTM_W, MLSTM_W, MLSTM_W, MLSTM_W, LANES)
    dtypes = (F32, F32, BF16, F32, BF16, F32, F32)
    return pl.pallas_call(
        _inproj_body,
        grid=(bsz, s // t),
        in_specs=[
            pl.BlockSpec((None, t, D_MODEL), cur),
            pl.BlockSpec((None, IN_HALO, D_MODEL), prev),
            _resident((None, 1, D_MODEL), lsel),
            _resident((None, D_MODEL, _M_IF), lsel),
            _resident((None, 1, _M_IF), lsel),
            _resident((None, D_MODEL, LANES), lambda b, i: (l, 0, _W_IF // LANES)),
            _resident((None, 1, LANES), lambda b, i: (l, 0, _W_IF // LANES)),
            _resident((None, CONV_K, CONV_W), lsel),
            _resident((None, 1, CONV_W), lsel),
            _resident((None, MLSTM_CONV_K, 2 * MLSTM_W), lsel),
            _resident((None, 1, 2 * MLSTM_W), lsel),
        ],
        out_specs=[pl.BlockSpec((None, t, w), cur) for w in widths],
        out_shape=[jax.ShapeDtypeStruct((bsz, s, w), dt) for w, dt in zip(widths, dtypes)],
        scratch_shapes=[
            pltpu.VMEM((IN_HALO + t, D_MODEL), BF16),
            pltpu.VMEM((IN_HALO + t, CONV_W), F32),
            pltpu.VMEM((2, SUBLANES, t + (CONV_K - 1) // SUBLANES * SUBLANES, LANES), F32),
            pltpu.VMEM((2, QK_HALO + t, MLSTM_W), F32),
        ],
        compiler_params=_params(2),
        name="inproj",
    )(h, h, gmix, w_all, b_all, w_all, b_all, conv_w, conv_b, mconv_w, mconv_b)


def _branches_body(h_ref, g_ref, cv_ref, pu_ref, puprev_ref,
                   lng_ref, lnb_ref, wco_ref,
                   pw_ref, ps_ref, wpo_ref, wg_ref, bg_ref,
                   yab_ref, extp, pools):
    t = pu_ref.shape[0]
    i = pl.program_id(1)
    has_prev = i > 0

    cv = cv_ref[...]
    mu = jnp.mean(cv, axis=-1, keepdims=True)
    var = jnp.mean(jnp.square(cv - mu), axis=-1, keepdims=True)
    a = (cv - mu) * lax.rsqrt(var + EPS) * lng_ref[...] + lnb_ref[...]
    a = a * _sigmoid(a)
    ya = _dot(a.astype(BF16), wco_ref[...])

    extp[0:POOL_HALO, :] = jnp.where(has_prev, puprev_ref[...], 0.0)
    extp[POOL_HALO:POOL_HALO + t, :] = pu_ref[...]
    for g, window in enumerate(POOL_WINDOWS):
        c0 = g * POOL_GW
        for r0 in range(0, t, VPU_ROWS):
            own = extp[POOL_HALO + r0:POOL_HALO + r0 + VPU_ROWS, c0:c0 + POOL_GW]
            acc = own
            for k in range(1, window):
                acc = acc + extp[POOL_HALO + r0 - k:POOL_HALO + r0 - k + VPU_ROWS, c0:c0 + POOL_GW]
            pos = i * t + r0 + lax.broadcasted_iota(jnp.int32, (VPU_ROWS, POOL_GW), 0)
            cnt = jnp.minimum(pos + 1, window).astype(F32)
            pools[r0:r0 + VPU_ROWS, c0:c0 + POOL_GW] = acc / cnt - own
    yp = jnp.concatenate(
        [_dot(pools[:, g * POOL_GW:(g + 1) * POOL_GW].astype(BF16), pw_ref[g]) for g in range(len(POOL_WINDOWS))],
        axis=1) * ps_ref[...]
    yb = _dot(yp.astype(BF16), wpo_ref[...])

    xn = _rms(h_ref[...], g_ref[...]).astype(BF16)
    gates = _sigmoid(_dot(xn, wg_ref[...]) + bg_ref[...])
    yab_ref[...] = gates[:, :D_MODEL] * ya + gates[:, D_MODEL:] * yb


def _branches(h, gmix, cv, pu, ln_g, ln_b, w_conv_out, pool_w, pool_scale, w_pool_out, w_gate, b_gate, l):
    bsz, s, _ = h.shape
    t = ROW_TILE
    lsel3 = lambda b, i: (l, 0, 0)
    lsel4 = lambda b, i: (l, 0, 0, 0)
    cur = lambda b, i: (b, i, 0)
    prev_p = lambda b, i: (b, jnp.maximum(i * (t // POOL_HALO) - 1, 0), 0)
    return pl.pallas_call(
        _branches_body,
        grid=(bsz, s // t),
        in_specs=[
            pl.BlockSpec((None, t, D_MODEL), cur),
            _resident((None, 1, D_MODEL), lsel3),
            pl.BlockSpec((None, t, CONV_W), cur),
            pl.BlockSpec((None, t, POOL_W), cur),
            pl.BlockSpec((None, POOL_HALO, POOL_W), prev_p),
            _resident((None, 1, CONV_W), lsel3),
            _resident((None, 1, CONV_W), lsel3),
            _resident((None, CONV_W, D_MODEL), lsel3),
            _resident((None, len(POOL_WINDOWS), POOL_GW, POOL_GW), lsel4),
            _resident((None, 1, POOL_W), lsel3),
            _resident((None, POOL_W, D_MODEL), lsel3),
            _resident((None, D_MODEL, 2 * D_MODEL), lambda b, i: (l, 0, _W_G01 // (2 * D_MODEL))),
            _resident((None, 1, 2 * D_MODEL), lambda b, i: (l, 0, _W_G01 // (2 * D_MODEL))),
        ],
        out_specs=pl.BlockSpec((None, t, D_MODEL), cur),
        out_shape=jax.ShapeDtypeStruct((bsz, s, D_MODEL), F32),
        scratch_shapes=[
            pltpu.VMEM((POOL_HALO + t, POOL_W), F32),
            pltpu.VMEM((t, POOL_W), F32),
        ],
        compiler_params=_params(2),
        name="branches",
    )(h, gmix, cv, pu, pu, ln_g, ln_b, w_conv_out, pool_w, pool_scale, w_pool_out, w_gate, b_gate)


def _cumsum_rows(x):
    n = x.shape[0]
    row = lax.broadcasted_iota(jnp.int32, x.shape, 0)
    shift = 1
    while shift < n:
        x = x + jnp.where(row >= shift, pltpu.roll(x, shift, axis=0), 0.0)
        shift *= 2
    return x


def _mlstm_body(q_ref, k_ref, v_ref, z_ref, if_ref, ng_ref, hz_ref, ct_scr, n_scr, m_scr):
    ln = q_ref.shape[0]
    c = pl.program_id(1)

    @pl.when(c == 0)
    def _():
        ct_scr[...] = jnp.zeros_like(ct_scr)
        n_scr[...] = jnp.zeros_like(n_scr)
        m_scr[...] = jnp.zeros_like(m_scr)

    gates = if_ref[...]
    logf = jnp.minimum(gates, 0.0) - jnp.log(1.0 + jnp.exp(-jnp.abs(gates)))
    bcum = _cumsum_rows(logf)
    lane = lax.broadcasted_iota(jnp.int32, gates.shape, 1)
    rows_t = jnp.where(lane < HEADS, gates, bcum).T

    tt = lax.broadcasted_iota(jnp.int32, (ln, ln), 0)
    ss = lax.broadcasted_iota(jnp.int32, (ln, ln), 1)
    causal = ss <= tt

    for hd in range(HEADS):
        qb = q_ref[:, hd * DH:(hd + 1) * DH]
        q = qb.astype(F32)
        k = k_ref[:, hd * DH:(hd + 1) * DH] * (DH ** -0.5)
        v = v_ref[:, hd * DH:(hd + 1) * DH]
        i_col = gates[:, hd:hd + 1]
        b_col = bcum[:, HEADS + hd:HEADS + hd + 1]
        i_row = rows_t[hd:hd + 1, :]
        b_row = rows_t[HEADS + hd:HEADS + hd + 1, :]
        m_prev = m_scr[hd][:, 0:1]
        ct = ct_scr[hd]
        nvec = n_scr[hd]

        log_d = jnp.where(causal, b_col - b_row + i_row, NEG)
        inter = b_col + m_prev
        mt = jnp.maximum(inter, jnp.max(log_d, axis=-1, keepdims=True))
        dmat = jnp.exp(log_d - mt)
        a_inter = jnp.exp(inter - mt)
        kb = k.astype(BF16)
        sc = lax.dot_general(qb, kb, (((1,), (1,)), ((), ())), preferred_element_type=F32) * dmat
        num = _dot(sc.astype(BF16), v) + a_inter * _dot(qb, ct.astype(BF16))
        den = jnp.sum(sc, axis=-1, keepdims=True) + a_inter * jnp.sum(q * nvec, axis=-1, keepdims=True)
        hh = num / jnp.maximum(jnp.abs(den), jnp.exp(-mt))

        b_last = b_col[ln - 1:ln, :]
        w_log = b_last - b_col + i_col
        m_new = jnp.maximum(b_last + m_prev, jnp.max(w_log, axis=0, keepdims=True))
        a_c = jnp.exp(b_last + m_prev - m_new)
        kw = k * jnp.exp(w_log - m_new)
        ct_scr[hd] = a_c * ct + lax.dot_general(kw.astype(BF16), v, (((0,), (0,)), ((), ())),
                                                preferred_element_type=F32)
        n_scr[hd] = a_c * nvec + jnp.sum(kw, axis=0, keepdims=True)
        m_scr[hd] = jnp.broadcast_to(m_new, (1, LANES))

        hn = hh * lax.rsqrt(jnp.mean(hh * hh, axis=-1, keepdims=True) + EPS) * ng_ref[:, hd * DH:(hd + 1) * DH]
        hz_ref[:, hd * DH:(hd + 1) * DH] = (hn * _sigmoid(z_ref[:, hd * DH:(hd + 1) * DH])).astype(BF16)


def _mlstm(q, k, v, z, gates_if, norm_g, l):
    bsz, s, _ = q.shape
    ln = MLSTM_CHUNK
    lsel3 = lambda b, c: (l, 0, 0)
    cur = lambda b, c: (b, c, 0)
    wide = pl.BlockSpec((None, ln, MLSTM_W), cur)
    return pl.pallas_call(
        _mlstm_body,
        grid=(bsz, s // ln),
        in_specs=[wide, wide, wide, wide, pl.BlockSpec((None, ln, LANES), cur), _resident((None, 1, MLSTM_W), lsel3)],
        out_specs=wide,
        out_shape=jax.ShapeDtypeStruct((bsz, s, MLSTM_W), BF16),
        scratch_shapes=[
            pltpu.VMEM((HEADS, DH, DH), F32),
            pltpu.VMEM((HEADS, 1, DH), F32),
            pltpu.VMEM((HEADS, 1, LANES), F32),
        ],
        compiler_params=_params(2),
        name="mlstm",
    )(q, k, v, z, gates_if, norm_g)


def _merge_xattn_body(with_router, h_ref, gmix_ref, yab_ref, hz_ref, wg_ref, bg_ref, wmo_ref, wout_ref,
                      gxa_ref, wq_ref, k_ref, v_ref, wo_ref, *rest):
    if with_router:
        gffn_ref, wr_ref, h2_ref, xf_ref, lg_ref = rest
    else:
        (h2_ref,) = rest
    h = h_ref[...]
    xn = _rms(h, gmix_ref[...]).astype(BF16)
    g2 = _sigmoid(_dot(xn, wg_ref[...]) + bg_ref[...])
    y = yab_ref[...] + g2 * _dot(hz_ref[...], wmo_ref[...])
    h1 = h + _dot(y.astype(BF16), wout_ref[...])

    hn = _rms(h1, gxa_ref[...]).astype(BF16)
    q = _dot(hn, wq_ref[...])
    heads = []
    for hd in range(XA_HEADS):
        sl = slice(hd * XA_DH, (hd + 1) * XA_DH)
        sc = lax.dot_general(q[:, sl].astype(BF16), k_ref[:, sl], (((1,), (1,)), ((), ())),
                             preferred_element_type=F32) * (XA_DH ** -0.5)
        e = jnp.exp(sc - jnp.max(sc, axis=-1, keepdims=True))
        p = e / jnp.sum(e, axis=-1, keepdims=True)
        heads.append(_dot(p.astype(BF16), v_ref[:, sl]))
    o = jnp.concatenate(heads, axis=1)
    h2 = h1 + _dot(o.astype(BF16), wo_ref[...])
    h2_ref[...] = h2
    if with_router:
        xf = _rms(h2, gffn_ref[...])
        xf_ref[...] = xf
        t = xf.shape[0]
        xf_hi = xf.astype(BF16)
        xf_lo = (xf - xf_hi.astype(F32)).astype(BF16)
        r = _dot(jnp.concatenate([xf_hi, xf_lo], axis=0), wr_ref[...])
        lg_ref[...] = r[:t, :LANES] + (r[t:, :LANES] + r[:t, LANES:])


def _merge_xattn(h, gmix, yab, hz, w_all, b_all, w_mo, w_out, gxa, wq, kmem, vmem, wo, l,
                 router=None):
    bsz, s, _ = h.shape
    t = ROW_TILE
    m = kmem.shape[2]
    lsel3 = lambda b, i: (l, 0, 0)
    cur = lambda b, i: (b, i, 0)
    mem = lambda b, i: (l, b, 0, 0)
    gsel = lambda b, i: (l, 0, _W_G2 // D_MODEL)
    sq = (None, D_MODEL, D_MODEL)
    in_specs = [
        pl.BlockSpec((None, t, D_MODEL), cur),
        _resident((None, 1, D_MODEL), lsel3),
        pl.BlockSpec((None, t, D_MODEL), cur),
        pl.BlockSpec((None, t, D_MODEL), cur),
        _resident(sq, gsel),
        _resident((None, 1, D_MODEL), gsel),
        _resident(sq, lsel3),
        _resident(sq, lsel3),
        _resident((None, 1, D_MODEL), lsel3),
        _resident(sq, lsel3),
        pl.BlockSpec((None, None, m, D_MODEL), mem),
        pl.BlockSpec((None, None, m, D_MODEL), mem),
        _resident(sq, lsel3),
    ]
    args = [h, gmix, yab, hz, w_all, b_all, w_mo, w_out, gxa, wq, kmem, vmem, wo]
    out_specs = [pl.BlockSpec((None, t, D_MODEL), cur)]
    out_shape = [jax.ShapeDtypeStruct((bsz, s, D_MODEL), F32)]
    if router is not None:
        gffn, w_router, j = router
        in_specs += [_resident((None, 1, D_MODEL), lsel3),
                     _resident((None, D_MODEL, 2 * LANES), lambda b, i: (j, 0, 0))]
        args += [gffn, w_router]
        out_specs += [pl.BlockSpec((None, t, D_MODEL), cur), pl.BlockSpec((None, t, LANES), cur)]
        out_shape += [jax.ShapeDtypeStruct((bsz, s, D_MODEL), F32), jax.ShapeDtypeStruct((bsz, s, LANES), F32)]
    return pl.pallas_call(
        functools.partial(_merge_xattn_body, router is not None),
        grid=(bsz, s // t),
        in_specs=in_specs,
        out_specs=out_specs,
        out_shape=out_shape,
        compiler_params=_params(2),
        name="merge_xattn_router" if router is not None else "merge_xattn",
    )(*args)


def _memkv_body(mem_ref, g_ref, wk_ref, wv_ref, k_ref, v_ref):
    mn = _rms(mem_ref[...], g_ref[...]).astype(BF16)
    k_ref[...] = _dot(mn, wk_ref[...]).astype(BF16)
    v_ref[...] = _dot(mn, wv_ref[...]).astype(BF16)


def _memkv(mem, g, wk, wv):
    bsz, m, _ = mem.shape
    depth = wk.shape[0]
    out = jax.ShapeDtypeStruct((depth, bsz, m, D_MODEL), BF16)
    return pl.pallas_call(
        _memkv_body,
        grid=(depth, bsz),
        in_specs=[
            pl.BlockSpec((None, m, D_MODEL), lambda l, b: (b, 0, 0)),
            pl.BlockSpec((1, D_MODEL), lambda l, b: (0, 0)),
            pl.BlockSpec((None, D_MODEL, D_MODEL), lambda l, b: (l, 0, 0)),
            pl.BlockSpec((None, D_MODEL, D_MODEL), lambda l, b: (l, 0, 0)),
        ],
        out_specs=[pl.BlockSpec((None, None, m, D_MODEL), lambda l, b: (l, b, 0, 0))] * 2,
        out_shape=[out, out],
        compiler_params=_params(2),
        name="memkv",
    )(mem, g, wk, wv)


def _ffn_body(final_norm, h_ref, g_ref, w1_ref, w3_ref, w2_ref, *rest):
    if final_norm:
        gfin_ref, out_ref = rest
    else:
        (out_ref,) = rest
    h = h_ref[...]
    xn = _rms(h, g_ref[...]).astype(BF16)
    acc = h
    ff = w1_ref.shape[1]
    for f0 in range(0, ff, FF_CHUNK_DENSE):
        a = _dot(xn, w1_ref[:, f0:f0 + FF_CHUNK_DENSE])
        b = _dot(xn, w3_ref[:, f0:f0 + FF_CHUNK_DENSE])
        acc = acc + _dot((a * _sigmoid(a) * b).astype(BF16), w2_ref[f0:f0 + FF_CHUNK_DENSE, :])
    out_ref[...] = _rms(acc, gfin_ref[...]) if final_norm else acc


def _ffn(h, gffn, w1, w3, w2, l, j, final_g=None):
    n = h.shape[0]
    t = ROW_TILE
    ff = w1.shape[2]
    row = lambda i: (i, 0)
    in_specs = [
        pl.BlockSpec((t, D_MODEL), row),
        _resident((None, 1, D_MODEL), lambda i: (l, 0, 0)),
        _resident((None, D_MODEL, ff), lambda i: (j, 0, 0)),
        _resident((None, D_MODEL, ff), lambda i: (j, 0, 0)),
        _resident((None, ff, D_MODEL), lambda i: (j, 0, 0)),
    ]
    args = [h, gffn, w1, w3, w2]
    if final_g is not None:
        in_specs.append(_resident((1, D_MODEL), lambda i: (0, 0)))
        args.append(final_g)
    return pl.pallas_call(
        functools.partial(_ffn_body, final_g is not None),
        grid=(n // t,),
        in_specs=in_specs,
        out_specs=pl.BlockSpec((t, D_MODEL), row),
        out_shape=jax.ShapeDtypeStruct((n, D_MODEL), F32),
        compiler_params=_params(1),
        name="ffn_dense",
    )(*args)


_R_E1, _R_E2, _R_G1, _R_G2, _R_RANK1, _R_RANK2 = range(6)


def _route_body(lg_ref, info_ref, cnt_ref, carry):
    t = lg_ref.shape[0]
    i = pl.program_id(0)

    @pl.when(i == 0)
    def _():
        carry[...] = jnp.zeros_like(carry)

    lane = lax.broadcasted_iota(jnp.int32, (t, LANES), 1)
    lanef = lane.astype(F32)
    lg = jnp.where(lane < N_EXPERTS, lg_ref[...], NEG)
    m1 = jnp.max(lg, axis=-1, keepdims=True)
    e1 = jnp.min(jnp.where(lg == m1, lanef, float(LANES)), axis=-1, keepdims=True)
    sel1 = lanef == e1
    lg2 = jnp.where(sel1, NEG, lg)
    m2 = jnp.max(lg2, axis=-1, keepdims=True)
    e2 = jnp.min(jnp.where(lg2 == m2, lanef, float(LANES)), axis=-1, keepdims=True)
    sel2 = lanef == e2
    ex = jnp.exp(m2 - m1)
    g1 = 1.0 / (1.0 + ex)
    g2 = ex / (1.0 + ex)

    onehot = jnp.where(sel1 | sel2, 1.0, 0.0)
    rr = lax.broadcasted_iota(jnp.int32, (t, t), 0)
    cc = lax.broadcasted_iota(jnp.int32, (t, t), 1)
    before = jnp.where(cc < rr, 1.0, 0.0).astype(BF16)
    pref = _dot(before, onehot.astype(BF16)) + carry[...]
    r1 = jnp.sum(jnp.where(sel1, pref, 0.0), axis=-1, keepdims=True)
    r2 = jnp.sum(jnp.where(sel2, pref, 0.0), axis=-1, keepdims=True)
    carry[...] = carry[...] + jnp.sum(onehot, axis=0, keepdims=True)

    info = jnp.zeros((t, LANES), F32)
    for col, val in ((_R_E1, e1), (_R_E2, e2), (_R_G1, g1), (_R_G2, g2), (_R_RANK1, r1), (_R_RANK2, r2)):
        info = jnp.where(lane == col, val, info)
    info_ref[...] = info
    cnt_ref[...] = carry[...]


def _route(logits):
    n = logits.shape[0]
    t = ROUTE_TILE
    return pl.pallas_call(
        _route_body,
        grid=(n // t,),
        in_specs=[pl.BlockSpec((t, LANES), lambda i: (i, 0))],
        out_specs=[pl.BlockSpec((t, LANES), lambda i: (i, 0)), pl.BlockSpec((1, LANES), lambda i: (0, 0))],
        out_shape=[jax.ShapeDtypeStruct((n, LANES), F32), jax.ShapeDtypeStruct((1, LANES), F32)],
        scratch_shapes=[pltpu.VMEM((1, LANES), F32)],
        compiler_params=_params(1),
        name="moe_route",
    )(logits)


def _moe_ffn_body(j, be_ref, nv_ref, stok_ref, sdst_ref, xf_hbm, w1_hbm, w3_hbm, w2_hbm, ys_hbm,
                  xbuf, ybuf, wb1, wb3, wb2, stage1, stage3, stage2, gsem, ssem, wsem):
    b = pl.program_id(0)
    last_blk = pl.num_programs(0) - 1
    nv = nv_ref[0]
    cur = b % 2
    nxt = 1 - cur
    ff = wb1.shape[1]
    n_chunks = ff // FF_CHUNK_MOE

    def gather_row(blk, buf, r):
        return pltpu.make_async_copy(xf_hbm.at[pl.ds(stok_ref[blk * MOE_BLOCK + r], 1)],
                                     xbuf.at[buf, pl.ds(r, 1)], gsem.at[buf])

    def scatter_row(blk, buf, r):
        return pltpu.make_async_copy(ybuf.at[buf, pl.ds(r, 1)],
                                     ys_hbm.at[pl.ds(sdst_ref[(blk + 1) * MOE_BLOCK + r], 1)], ssem.at[buf])

    def rolled(fn):
        def body(r, carry):
            fn(r)
            return carry
        lax.fori_loop(0, MOE_BLOCK, body, 0, unroll=DMA_UNROLL)

    def inline(fn, rows=range(MOE_BLOCK)):
        for r in rows:
            fn(r)

    @pl.when(b == 0)
    def _():
        rolled(lambda r: gather_row(0, 0, r).start())
        ybuf[1] = jnp.zeros((MOE_BLOCK, D_MODEL), F32)

    @pl.when(b < nv)
    def _():
        expert = be_ref[b]
        new_expert = jnp.logical_or(b == 0, expert != be_ref[jnp.maximum(b - 1, 0)])

        def weight_chunk(f, start):
            cols = pl.ds(f * FF_CHUNK_MOE, FF_CHUNK_MOE)
            slot = f % 2
            for src, dst in ((w1_hbm.at[j, expert, :, cols], stage1.at[slot]),
                             (w3_hbm.at[j, expert, :, cols], stage3.at[slot]),
                             (w2_hbm.at[j, expert, cols, :], stage2.at[slot])):
                if start:
                    pltpu.async_copy(src, dst, wsem.at[slot], priority=1)
                else:
                    pltpu.make_async_copy(src, dst, wsem.at[slot]).wait()

        inline(lambda r: gather_row(b, cur, r).wait())

        @pl.when(b >= 1)
        def _():
            inline(lambda r: scatter_row(b - 2, cur, r).wait())

        @pl.when(new_expert)
        def _():
            weight_chunk(0, True)

        x = xbuf[cur].astype(BF16)
        ahead = jnp.minimum(b + 1, last_blk)
        per_chunk = -(-MOE_BLOCK // n_chunks)
        for f in range(n_chunks):
            fcols = slice(f * FF_CHUNK_MOE, (f + 1) * FF_CHUNK_MOE)

            @pl.when(new_expert)
            def _():
                if f + 1 < n_chunks:
                    weight_chunk(f + 1, True)
                weight_chunk(f, False)
                wb1[:, fcols] = stage1[f % 2].astype(BF16)
                wb3[:, fcols] = stage3[f % 2].astype(BF16)
                wb2[fcols, :] = stage2[f % 2].astype(BF16)

            rows = range(f * per_chunk, min((f + 1) * per_chunk, MOE_BLOCK))
            inline(lambda r: gather_row(ahead, nxt, r).start(), rows)
            inline(lambda r: scatter_row(b - 1, nxt, r).start(), rows)
            a = _dot(x, wb1[:, fcols])
            g = _dot(x, wb3[:, fcols])
            part = _dot((a * _sigmoid(a) * g).astype(BF16), wb2[fcols, :])
            if f == 0:
                ybuf[cur] = part
            else:
                ybuf[cur] = ybuf[cur] + part

        @pl.when(b == nv - 1)
        def _():
            rolled(lambda r: gather_row(ahead, nxt, r).wait())
            rolled(lambda r: scatter_row(b, cur, r).start())
            rolled(lambda r: scatter_row(b - 1, nxt, r).wait())
            rolled(lambda r: scatter_row(b, cur, r).wait())

    @pl.when(b >= nv)
    def _():
        ybuf[0] = jnp.zeros((MOE_BLOCK, D_MODEL), F32)
        rolled(lambda r: scatter_row(b, 0, r).start())
        rolled(lambda r: scatter_row(b, 0, r).wait())


def _moe_ffn(block_e, n_valid, slot_tok, slot_dst, xf, w1, w3, w2, j):
    p = slot_tok.shape[0]
    ff = w1.shape[3]
    spare = p + jnp.arange(MOE_BLOCK, dtype=jnp.int32)
    slot_dst = jnp.concatenate([spare, slot_dst])
    any_spec = pl.BlockSpec(memory_space=pl.ANY)
    return pl.pallas_call(
        functools.partial(_moe_ffn_body, j),
        grid_spec=pltpu.PrefetchScalarGridSpec(
            num_scalar_prefetch=4,
            grid=(p // MOE_BLOCK,),
            in_specs=[any_spec] * 4,
            out_specs=any_spec,
            scratch_shapes=[
                pltpu.VMEM((2, MOE_BLOCK, D_MODEL), F32),
                pltpu.VMEM((2, MOE_BLOCK, D_MODEL), F32),
                pltpu.VMEM((D_MODEL, ff), BF16),
                pltpu.VMEM((D_MODEL, ff), BF16),
                pltpu.VMEM((ff, D_MODEL), BF16),
                pltpu.VMEM((2, D_MODEL, FF_CHUNK_MOE), F32),
                pltpu.VMEM((2, D_MODEL, FF_CHUNK_MOE), F32),
                pltpu.VMEM((2, FF_CHUNK_MOE, D_MODEL), F32),
                pltpu.SemaphoreType.DMA((2,)),
                pltpu.SemaphoreType.DMA((2,)),
                pltpu.SemaphoreType.DMA((2,)),
            ],
        ),
        out_shape=jax.ShapeDtypeStruct((p + MOE_BLOCK, D_MODEL), F32),
        compiler_params=pltpu.CompilerParams(dimension_semantics=("arbitrary",), vmem_limit_bytes=VMEM_LIMIT,
                                             has_side_effects=True),
        name="moe_ffn",
    )(block_e, n_valid, slot_tok, slot_dst, xf, w1, w3, w2)


def _combine_body(final_norm, h_ref, info_ref, y1_ref, y2_ref, *rest):
    if final_norm:
        gfin_ref, out_ref = rest
    else:
        (out_ref,) = rest
    info = info_ref[...]
    out = h_ref[...] + (info[:, _R_G1:_R_G1 + 1] * y1_ref[...] + info[:, _R_G2:_R_G2 + 1] * y2_ref[...])
    out_ref[...] = _rms(out, gfin_ref[...]) if final_norm else out


def _combine(h, info, ys, final_g=None):
    n = h.shape[0]
    t = ROW_TILE
    nt = n // t
    row = lambda i: (i, 0)
    in_specs = [
        pl.BlockSpec((t, D_MODEL), row),
        pl.BlockSpec((t, LANES), row),
        pl.BlockSpec((t, D_MODEL), row),
        pl.BlockSpec((t, D_MODEL), lambda i: (i + nt, 0)),
    ]
    args = [h, info, ys, ys]
    if final_g is not None:
        in_specs.append(_resident((1, D_MODEL), lambda i: (0, 0)))
        args.append(final_g)
    return pl.pallas_call(
        functools.partial(_combine_body, final_g is not None),
        grid=(nt,),
        in_specs=in_specs,
        out_specs=pl.BlockSpec((t, D_MODEL), row),
        out_shape=jax.ShapeDtypeStruct((n, D_MODEL), F32),
        compiler_params=_params(1),
        name="moe_combine",
    )(*args)


def _moe(h2, xf, logits, w1, w3, w2, j, final_g=None):
    n = h2.shape[0]
    info, counts = _route(logits)
    cnt = counts[0, :N_EXPERTS].astype(jnp.int32)
    padded = (cnt + MOE_BLOCK - 1) // MOE_BLOCK * MOE_BLOCK
    ends = jnp.cumsum(padded)
    starts = ends - padded
    n_blocks = (n * TOP_K) // MOE_BLOCK + N_EXPERTS
    block_e = jnp.minimum(jnp.searchsorted(ends, jnp.arange(n_blocks, dtype=jnp.int32) * MOE_BLOCK, side="right"),
                          N_EXPERTS - 1).astype(jnp.int32)
    n_valid = (ends[-1:] // MOE_BLOCK).astype(jnp.int32)
    e1 = info[:, _R_E1].astype(jnp.int32)
    e2 = info[:, _R_E2].astype(jnp.int32)
    slot = jnp.concatenate([starts[e1] + info[:, _R_RANK1].astype(jnp.int32),
                            starts[e2] + info[:, _R_RANK2].astype(jnp.int32)])
    n_asg = n * TOP_K
    p = n_blocks * MOE_BLOCK
    asg_of_slot = jnp.full((p,), -1, jnp.int32).at[slot].set(jnp.arange(n_asg, dtype=jnp.int32))
    is_pad = asg_of_slot < 0
    slot_tok = jnp.where(is_pad, 0, asg_of_slot % n)
    slot_dst = jnp.where(is_pad, n_asg - 1 + jnp.cumsum(is_pad.astype(jnp.int32)), asg_of_slot)
    ys = _moe_ffn(block_e, n_valid, slot_tok, slot_dst, xf, w1, w3, w2, j)
    return _combine(h2, info, ys, final_g)


def _row3(a):
    return a.reshape(a.shape[0], 1, a.shape[1])


def kernel(x, mem, mem_norm_g, norm_mix_g, w_in, b_in, conv_dw_w, conv_dw_b, conv_ln_g, conv_ln_b, w_conv_out,
           pool_w, pool_scale, w_pool_out, mlstm_conv_w, mlstm_conv_b, mlstm_norm_g, w_mlstm_out, w_out,
           norm_xattn_g, xa_wq, xa_wk, xa_wv, xa_wo, norm_ffn_g, ffn_w1, ffn_w3, ffn_w2, router_w,
           moe_w1, moe_w3, moe_w2, final_norm_g):
    bsz, s, d = x.shape
    n = bsz * s
    depth = w_in.shape[0]
    bf = lambda a: a.astype(BF16)

    def reorder(a):
        gap = ((0, 0), (0, 0), (0, _W_G01 - _W_IF - 2 * HEADS))
        return jnp.concatenate([a[:, :, :_M_IF], jnp.pad(a[:, :, _M_IF:_G_PRE], gap), a[:, :, _G_PRE:]], axis=2)

    w_all = bf(reorder(w_in))
    b_all = reorder(_row3(b_in))
    gmix = _row3(norm_mix_g)
    gxa = _row3(norm_xattn_g)
    gffn = _row3(norm_ffn_g)
    final_g = final_norm_g.reshape(1, d)
    conv_b3, ln_g3, ln_b3 = _row3(conv_dw_b), _row3(conv_ln_g), _row3(conv_ln_b)
    w_co, w_po, pool_wb = bf(w_conv_out), bf(w_pool_out), bf(pool_w)
    pool_s3 = _row3(pool_scale)
    mconv_b3, mnorm_g3 = _row3(mlstm_conv_b), _row3(mlstm_norm_g)
    w_mo, w_o = bf(w_mlstm_out), bf(w_out)
    wq, wk, wv, wo = bf(xa_wq), bf(xa_wk), bf(xa_wv), bf(xa_wo)
    f1, f3, f2 = bf(ffn_w1), bf(ffn_w3), bf(ffn_w2)
    w_router = jnp.pad(router_w, ((0, 0), (0, 0), (0, LANES - N_EXPERTS)))
    w_router_hi = bf(w_router)
    w_router_hl = jnp.concatenate([w_router_hi, bf(w_router - w_router_hi.astype(F32))], axis=2)

    kmem, vmem = _memkv(mem, mem_norm_g.reshape(1, d), wk, wv)

    h = x
    for l in range(depth):
        last = l == depth - 1
        cv, pu, q, k, v, z, gif = _inproj(h, gmix, w_all, b_all, conv_dw_w, conv_b3,
                                          mlstm_conv_w, mconv_b3, l)
        yab = _branches(h, gmix, cv, pu, ln_g3, ln_b3, w_co, pool_wb, pool_s3, w_po, w_all, b_all, l)
        hz = _mlstm(q, k, v, z, gif, mnorm_g3, l)
        j = l // 2
        if l % 2 == 0:
            (h2,) = _merge_xattn(h, gmix, yab, hz, w_all, b_all, w_mo, w_o, gxa, wq, kmem, vmem, wo, l)
            hnext = _ffn(h2.reshape(n, d), gffn, f1, f3, f2, l, j, final_g if last else None)
        else:
            h2, xf, logits = _merge_xattn(h, gmix, yab, hz, w_all, b_all, w_mo, w_o, gxa, wq, kmem, vmem, wo, l,
                                          router=(gffn, w_router_hl, j))
            hnext = _moe(h2.reshape(n, d), xf.reshape(n, d), logits.reshape(n, LANES), moe_w1, moe_w3, moe_w2, j,
                         final_g if last else None)
        h = hnext.reshape(bsz, s, d)
    return h
```
